```python
import jax, jax.numpy as jnp
from jax import lax
import numpy as np

D_MODEL = 1024
BATCH = 32
SEQ = 256
DEPTH = 4
DEC_BATCH = 4
DEC_SEQ = 4096
PAST_LEN = 512

GRID_W = 64
N_MIXERS = 2
N_HGRN = (DEPTH + 1) // 2
N_CONV = DEPTH // 2
HGRN_HEADS = 8
HGRN_KEY_DIM = D_MODEL // HGRN_HEADS
HGRN_VAL_DIM = D_MODEL // HGRN_HEADS
CHUNK = 32
CONV_K = 31
D_FF = 4 * D_MODEL
N_MOD = 6
EPS = 1e-6
K_MAX = 1.0 - 1e-6

kernel_name = 'hybrid_hgrn2_conformer_dit_step'

F32 = jnp.float32


def rmsnorm(x, w):
    xf = x.astype(F32)
    y = xf * lax.rsqrt(jnp.mean(xf * xf, axis=-1, keepdims=True) + EPS)
    return (y * w.astype(F32)).astype(x.dtype)


def grid_pos_embed(n_tok, dim):
    rows = n_tok // GRID_W
    rr, cc = jnp.meshgrid(jnp.arange(rows, dtype=F32), jnp.arange(GRID_W, dtype=F32), indexing='ij')
    nf = dim // 4
    omega = 1.0 / (10000.0 ** (jnp.arange(nf, dtype=F32) / nf))
    er = rr.reshape(-1)[:, None] * omega
    ec = cc.reshape(-1)[:, None] * omega
    return jnp.concatenate([jnp.sin(er), jnp.cos(er), jnp.sin(ec), jnp.cos(ec)], axis=-1)


def chunk_gla_scan(q, k, v, log_g, s0):
    B, T, H, K = q.shape
    V = v.shape[-1]
    nc = T // CHUNK

    def to_chunks(a):
        return a.reshape(B, nc, CHUNK, H, a.shape[-1]).transpose(1, 0, 3, 2, 4)

    tril = jnp.tril(jnp.ones((CHUNK, CHUNK), dtype=bool))[:, :, None]

    def step(S, inp):
        qc, kc, vc, gc = inp
        b = jnp.cumsum(gc, axis=2)
        diff = b[:, :, :, None, :] - b[:, :, None, :, :]
        decay = jnp.where(tril, jnp.exp(jnp.where(tril, diff, 0.0)), 0.0)
        scores = jnp.einsum('bhtk,bhsk,bhtsk->bhts', qc, kc, decay)
        o = jnp.einsum('bhts,bhsv->bhtv', scores, vc) + jnp.einsum('bhtk,bhkv->bhtv', qc * jnp.exp(b), S)
        b_last = b[:, :, -1, :]
        S = jnp.exp(b_last)[..., None] * S + jnp.einsum('bhsk,bhsv->bhkv', kc * jnp.exp(b_last[:, :, None, :] - b), vc)
        return S, o

    S, o = lax.scan(step, s0, (to_chunks(q), to_chunks(k), to_chunks(v), to_chunks(log_g)))
    o = o.transpose(1, 0, 3, 2, 4).reshape(B, T, H, V)
    return o, S


def hgrn2_mixer(h, w_in, lb_f, lb_b, g_norm, w_out, s0):
    B, T, _ = h.shape
    proj = (h @ w_in).astype(F32)
    q, v, f_f, f_b, g = jnp.split(proj, 5, axis=-1)

    def heads(a):
        return a.reshape(B, T, HGRN_HEADS, -1)

    q = heads(jax.nn.silu(q) * (HGRN_KEY_DIM ** -0.5))
    v = heads(v)

    def gates(f, lb):
        lb = lb.astype(F32)
        k = jnp.minimum((1.0 - lb) * jax.nn.sigmoid(-f), K_MAX)
        log_g = jnp.log1p(-k)
        return heads(k), heads(log_g)

    k_f, lg_f = gates(f_f, lb_f)
    k_b, lg_b = gates(f_b, lb_b)
    s0 = s0.astype(F32)
    o_f, S_f = chunk_gla_scan(q, k_f, v, lg_f, s0[:, 0])
    rev = lambda a: jnp.flip(a, axis=1)
    o_b, S_b = chunk_gla_scan(rev(q), rev(k_b), rev(v), rev(lg_b), s0[:, 1])
    o = o_f + rev(o_b)
    o = o * lax.rsqrt(jnp.mean(o * o, axis=-1, keepdims=True) + EPS)
    o = o.reshape(B, T, D_MODEL) * g_norm.astype(F32) * jax.nn.silu(g)
    out = o.astype(h.dtype) @ w_out
    return out, jnp.stack([S_f, S_b], axis=1)


def conformer_conv(h, w_pw1, w_dw, b_dw, ln_g, ln_b, w_pw2):
    u = h @ w_pw1
    a, gt = jnp.split(u, 2, axis=-1)
    u = a * jax.nn.sigmoid(gt)
    u = lax.conv_general_dilated(u, w_dw[:, None, :], window_strides=(1,),
                                 padding=[(CONV_K // 2, CONV_K // 2)],
                                 dimension_numbers=('NWC', 'WIO', 'NWC'),
                                 feature_group_count=D_MODEL) + b_dw
    uf = u.astype(F32)
    mu = jnp.mean(uf, axis=-1, keepdims=True)
    var = jnp.mean(jnp.square(uf - mu), axis=-1, keepdims=True)
    uf = (uf - mu) * lax.rsqrt(var + EPS) * ln_g.astype(F32) + ln_b.astype(F32)
    return jax.nn.silu(uf).astype(h.dtype) @ w_pw2


def sqrelu_mlp(h, w1, w2):
    return jnp.square(jax.nn.relu(h @ w1)) @ w2


def setup_inputs(seed: int = 0) -> dict:
    key = jax.random.key(seed)
    ks = jax.random.split(key, 24)

    def nrm(k, shape, s):
        return jax.random.normal(k, shape, F32) * s

    D = D_MODEL
    return {
        'x_prompt': nrm(ks[0], (BATCH, SEQ, D), 1.0),
        'x_sample': nrm(ks[1], (DEC_BATCH, DEC_SEQ, D), 1.0),
        'c': nrm(ks[2], (DEC_BATCH, D), 1.0),
        'state_hgrn': nrm(ks[3], (DEC_BATCH, N_HGRN, 2, HGRN_HEADS, HGRN_KEY_DIM, HGRN_VAL_DIM), 0.5),
        'c_ctx': nrm(ks[4], (D,), 1.0),
        'w_mod': nrm(ks[5], (DEPTH, D, N_MOD * D), 0.5 * D ** -0.5),
        'b_mod': nrm(ks[6], (DEPTH, N_MOD * D), 0.02),
        'norm_mix': 1.0 + nrm(ks[7], (DEPTH, D), 0.02),
        'norm_mlp': 1.0 + nrm(ks[8], (DEPTH, D), 0.02),
        'hgrn_w_in': nrm(ks[9], (N_HGRN, D, 5 * D), D ** -0.5),
        'hgrn_lb_fwd': nrm(ks[10], (N_HGRN, D), 1.0),
        'hgrn_lb_bwd': nrm(ks[11], (N_HGRN, D), 1.0),
        'hgrn_g_norm': 1.0 + nrm(ks[12], (N_HGRN, D), 0.02),
        'hgrn_w_out': nrm(ks[13], (N_HGRN, D, D), D ** -0.5),
        'conv_w_pw1': nrm(ks[14], (N_CONV, D, 2 * D), D ** -0.5),
        'conv_w_dw': nrm(ks[15], (N_CONV, CONV_K, D), CONV_K ** -0.5),
        'conv_b_dw': nrm(ks[16], (N_CONV, D), 0.02),
        'conv_ln_g': 1.0 + nrm(ks[17], (N_CONV, D), 0.02),
        'conv_ln_b': nrm(ks[18], (N_CONV, D), 0.02),
        'conv_w_pw2': nrm(ks[19], (N_CONV, D, D), D ** -0.5),
        'mlp_w1': nrm(ks[20], (DEPTH, D, D_FF), D ** -0.5),
        'mlp_w2': nrm(ks[21], (DEPTH, D_FF, D), D_FF ** -0.5),
        'final_norm': 1.0 + nrm(ks[22], (D,), 0.02),
    }


def reference(x_prompt, x_sample, c, state_hgrn, c_ctx, w_mod, b_mod, norm_mix, norm_mlp,
              hgrn_w_in, hgrn_lb_fwd, hgrn_lb_bwd, hgrn_g_norm, hgrn_w_out,
              conv_w_pw1, conv_w_dw, conv_b_dw, conv_ln_g, conv_ln_b, conv_w_pw2,
              mlp_w1, mlp_w2, final_norm):
    def lower_bounds(p):
        p = jax.nn.softmax(p.astype(F32), axis=0)
        return jnp.maximum(jnp.cumsum(p, axis=0) - p[0], 0.0)

    lbs_f = lower_bounds(hgrn_lb_fwd)
    lbs_b = lower_bounds(hgrn_lb_bwd)

    n_lat = x_sample.shape[1]
    xp = x_prompt
    xs = x_sample + grid_pos_embed(n_lat, D_MODEL).astype(x_sample.dtype)[None]
    zero_state = jnp.zeros((x_prompt.shape[0], 2, HGRN_HEADS, HGRN_KEY_DIM, HGRN_VAL_DIM), F32)
    new_states = []

    for i in range(DEPTH):
        m_ctx = jnp.split(jax.nn.silu(c_ctx) @ w_mod[i] + b_mod[i], N_MOD, axis=-1)
        m_lat = jnp.split((jax.nn.silu(c) @ w_mod[i] + b_mod[i])[:, None, :], N_MOD, axis=-1)
        hp = rmsnorm(xp, norm_mix[i]) * (1.0 + m_ctx[1]) + m_ctx[0]
        hs = rmsnorm(xs, norm_mix[i]) * (1.0 + m_lat[1]) + m_lat[0]
        if i % N_MIXERS == 0:
            a = i // N_MIXERS
            args = (hgrn_w_in[a], lbs_f[a], lbs_b[a], hgrn_g_norm[a], hgrn_w_out[a])
            mp, st = hgrn2_mixer(hp, *args, zero_state)
            new_states.append(st.astype(x_prompt.dtype))
            ms, _ = hgrn2_mixer(hs, *args, state_hgrn[:, a])
        else:
            b = i // N_MIXERS
            args = (conv_w_pw1[b], conv_w_dw[b], conv_b_dw[b], conv_ln_g[b], conv_ln_b[b], conv_w_pw2[b])
            mp = conformer_conv(hp, *args)
            ms = conformer_conv(hs, *args)
        xp = xp + m_ctx[2] * mp
        xs = xs + m_lat[2] * ms
        hp = rmsnorm(xp, norm_mlp[i]) * (1.0 + m_ctx[4]) + m_ctx[3]
        hs = rmsnorm(xs, norm_mlp[i]) * (1.0 + m_lat[4]) + m_lat[3]
        xp = xp + m_ctx[5] * sqrelu_mlp(hp, mlp_w1[i], mlp_w2[i])
        xs = xs + m_lat[5] * sqrelu_mlp(hs, mlp_w1[i], mlp_w2[i])

    y_prompt = rmsnorm(xp, final_norm)
    y_sample = rmsnorm(xs, final_norm)
    new_state_hgrn = jnp.stack(new_states, axis=1)
    return (y_prompt, y_sample, new_state_hgrn)
```

```python
import functools

import jax
import jax.numpy as jnp
from jax import lax
from jax.experimental import pallas as pl
from jax.experimental.pallas import tpu as pltpu

F32 = jnp.float32
BF16 = jnp.bfloat16

N_MOD = 6
HEADS = 8
HEAD_DIM = 128
GRID_W = 64
CONV_K = 31
EPS = 1e-6
K_MAX = 1.0 - 1e-6

CHUNK = 128
SUB = 16
SUBLANES = 8
HALO = 16
MOD_ROWS = 8

TM = 512
TM_CONV = 256
TF = 1024
VMEM_LIMIT = 56 * 1024 * 1024


def _params(*sem):
    return pltpu.CompilerParams(dimension_semantics=sem, vmem_limit_bytes=VMEM_LIMIT)


def _dot(a, b):
    return jnp.dot(a, b, preferred_element_type=F32)


def _dot_nt(a, b):
    return lax.dot_general(a, b, (((1,), (1,)), ((), ())), preferred_element_type=F32)


def _dot_tn(a, b):
    return lax.dot_general(a, b, (((0,), (0,)), ((), ())), preferred_element_type=F32)


def _silu(x):
    return x * jax.nn.sigmoid(x)


def _norm_mod(x, nw, shift, scale):
    ms = jnp.mean(x * x, axis=-1, keepdims=True)
    y = x * lax.rsqrt(ms + EPS) * nw
    return y * (1.0 + scale) + shift


def _mod_kernel(c_ref, w_ref, b_ref, o_ref):
    c = _silu(c_ref[...]).astype(BF16)
    o_ref[0] = _dot(c, w_ref[0].astype(BF16)) + b_ref[0]


def _modulation(cvec, w_mod, b_mod):
    depth, d, n = w_mod.shape
    tn = 1536
    return pl.pallas_call(
        _mod_kernel,
        grid=(depth, n // tn),
        in_specs=[
            pl.BlockSpec((MOD_ROWS, d), lambda l, j: (0, 0)),
            pl.BlockSpec((1, d, tn), lambda l, j: (l, 0, j)),
            pl.BlockSpec((1, 1, tn), lambda l, j: (l, 0, j)),
        ],
        out_specs=pl.BlockSpec((1, MOD_ROWS, tn), lambda l, j: (l, 0, j)),
        out_shape=jax.ShapeDtypeStruct((depth, MOD_ROWS, n), F32),
        compiler_params=_params("parallel", "parallel"),
        name="modulation",
    )(cvec, w_mod, b_mod.reshape(depth, 1, n))


def _mlp_kernel(x_ref, mod_ref, nw_ref, fw_ref, w1_ref, w2_ref, o_ref, hn_ref, acc_ref, *, final):
    j = pl.program_id(1)

    @pl.when(j == 0)
    def _():
        m = mod_ref[0]
        hn_ref[...] = _norm_mod(x_ref[...], nw_ref[...], m[3:4], m[4:5]).astype(BF16)
        acc_ref[...] = jnp.zeros_like(acc_ref)

    h = _dot(hn_ref[...], w1_ref[...])
    h = jnp.square(jnp.maximum(h, 0.0)).astype(BF16)
    acc_ref[...] += _dot(h, w2_ref[...])

    @pl.when(j == pl.num_programs(1) - 1)
    def _():
        y = x_ref[...] + mod_ref[0][5:6] * acc_ref[...]
        if final:
            ms = jnp.mean(y * y, axis=-1, keepdims=True)
            y = y * lax.rsqrt(ms + EPS) * fw_ref[...]
        o_ref[...] = y


def _split3(x):
    hi = x.astype(BF16)
    r = x - hi.astype(F32)
    mid = r.astype(BF16)
    lo = (r - mid.astype(F32)).astype(BF16)
    return hi, mid, lo


def _hgrn_in_kernel(x_ref, mod_ref, nw_ref, w_ref, lbf_ref, lbb_ref, gn_ref,
                    q_ref, v_ref, kf_ref, bf_ref, kb_ref, bb_ref, g_ref, hn_ref):
    j = pl.program_id(1)
    tm = x_ref.shape[0]

    @pl.when(j == 0)
    def _():
        m = mod_ref[0]
        hn_ref[...] = _norm_mod(x_ref[...], nw_ref[...], m[0:1], m[1:2]).astype(BF16)

    p = _dot(hn_ref[...], w_ref[...])

    def gates(lb_ref, k_ref, b_ref, reverse):
        k = jnp.minimum((1.0 - lb_ref[...]) * jax.nn.sigmoid(-p), K_MAX)
        k_ref[...] = k
        lg = jnp.log1p(-k)
        row = lax.broadcasted_iota(jnp.int32, (CHUNK, CHUNK), 0)
        col = lax.broadcasted_iota(jnp.int32, (CHUNK, CHUNK), 1)
        tri = jnp.where((col >= row) if reverse else (col <= row), 1.0, 0.0).astype(BF16)
        for r in range(tm // CHUNK):
            hi, mid, lo = _split3(lg[r * CHUNK:(r + 1) * CHUNK])
            b_ref[r * CHUNK:(r + 1) * CHUNK, :] = _dot(tri, hi) + _dot(tri, mid) + _dot(tri, lo)

    @pl.when(j == 0)
    def _():
        q_ref[...] = _silu(p) * (HEAD_DIM ** -0.5)

    @pl.when(j == 1)
    def _():
        v_ref[...] = p.astype(BF16)

    @pl.when(j == 2)
    def _():
        gates(lbf_ref, kf_ref, bf_ref, False)

    @pl.when(j == 3)
    def _():
        gates(lbb_ref, kb_ref, bb_ref, True)

    @pl.when(j == 4)
    def _():
        g_ref[...] = gn_ref[...] * _silu(p)


def _scan_core(q, k, b, v, st_ref, reverse):
    n_sub = CHUNK // SUB
    row = lax.broadcasted_iota(jnp.int32, (CHUNK, CHUNK), 0)
    col = lax.broadcasted_iota(jnp.int32, (CHUNK, CHUNK), 1)
    keep = (col >= row) if reverse else (col <= row)
    b_last = b[0:1] if reverse else b[CHUNK - 1:CHUNK]

    st = st_ref[...]
    o = _dot_nt((q * jnp.exp(b)).astype(BF16), st.astype(BF16))
    k_end = (k * jnp.exp(b_last - b)).astype(BF16)
    st_ref[...] = st * jnp.exp(b_last) + _dot_tn(v, k_end)

    blocks = []
    for i in range(n_sub):
        lo, hi = i * SUB, (i + 1) * SUB
        if reverse:
            ref = b[hi:hi + 1] if i < n_sub - 1 else jnp.zeros((1, HEAD_DIM), F32)
            s0, s1 = lo, CHUNK
        else:
            ref = b[lo - 1:lo] if i > 0 else jnp.zeros((1, HEAD_DIM), F32)
            s0, s1 = 0, hi
        qi = (q[lo:hi] * jnp.exp(b[lo:hi] - ref)).astype(BF16)
        ks = (k[s0:s1] * jnp.exp(ref - b[s0:s1])).astype(BF16)
        parts = []
        if s0 > 0:
            parts.append(jnp.zeros((s0, HEAD_DIM), BF16))
        parts.append(ks)
        if s1 < CHUNK:
            parts.append(jnp.zeros((CHUNK - s1, HEAD_DIM), BF16))
        ks = jnp.concatenate(parts, axis=0) if len(parts) > 1 else ks
        blocks.append(_dot_nt(qi, ks))
    a = jnp.where(keep, jnp.concatenate(blocks, axis=0), 0.0).astype(BF16)
    return o + _dot(a, v)


def _scan_kernel(qf_ref, kf_ref, bf_ref, vf_ref, qb_ref, kb_ref, bb_ref, vb_ref, s0_ref,
                 of_ref, ob_ref, sout_ref, st_ref, *, lat_chunks, lat_nc, ctx_nc):
    g = pl.program_id(0)
    is_lat = g < lat_chunks
    c = jnp.where(is_lat, g % lat_nc, (g - lat_chunks) % ctx_nc)
    last = c == jnp.where(is_lat, lat_nc, ctx_nc) - 1

    @pl.when(jnp.logical_and(c == 0, is_lat))
    def _():
        for d in range(2):
            for h in range(HEADS):
                st_ref[d, h] = s0_ref[0, 0, d, h].T

    @pl.when(jnp.logical_and(c == 0, jnp.logical_not(is_lat)))
    def _():
        st_ref[...] = jnp.zeros_like(st_ref)

    for h in range(HEADS):
        sl = slice(h * HEAD_DIM, (h + 1) * HEAD_DIM)
        of_ref[:, sl] = _scan_core(qf_ref[:, sl], kf_ref[:, sl], bf_ref[:, sl], vf_ref[:, sl],
                                   st_ref.at[0, h], False)
        ob_ref[:, sl] = _scan_core(qb_ref[:, sl], kb_ref[:, sl], bb_ref[:, sl], vb_ref[:, sl],
                                   st_ref.at[1, h], True)

    @pl.when(jnp.logical_and(last, jnp.logical_not(is_lat)))
    def _():
        for d in range(2):
            for h in range(HEADS):
                sout_ref[0, d, h] = st_ref[d, h].T


def _scan(q, kf, bf, kb, bb, v, s0, layer, *, lat_b, lat_t, ctx_b, ctx_t):
    n_tok, d = q.shape
    lat_nc, ctx_nc = lat_t // CHUNK, ctx_t // CHUNK
    lat_chunks = lat_b * lat_nc

    def seq_of(g):
        is_lat = g < lat_chunks
        cs = (g - lat_chunks) // ctx_nc
        start = jnp.where(is_lat, (g // lat_nc) * lat_nc, lat_chunks + cs * ctx_nc)
        return start, jnp.where(is_lat, lat_nc, ctx_nc), jnp.where(is_lat, 0, cs)

    def fwd(g):
        return (g, 0)

    def bwd(g):
        start, nc, _ = seq_of(g)
        return (2 * start + nc - 1 - g, 0)

    blk = lambda im: pl.BlockSpec((CHUNK, d), im)
    state_blk = (1, 2, HEADS, HEAD_DIM, HEAD_DIM)
    return pl.pallas_call(
        functools.partial(_scan_kernel, lat_chunks=lat_chunks, lat_nc=lat_nc, ctx_nc=ctx_nc),
        grid=(n_tok // CHUNK,),
        in_specs=[blk(fwd), blk(fwd), blk(fwd), blk(fwd), blk(bwd), blk(bwd), blk(bwd), blk(bwd),
                  pl.BlockSpec((1, 1) + state_blk[1:],
                               lambda g: (jnp.minimum(g // lat_nc, lat_b - 1), layer, 0, 0, 0, 0))],
        out_specs=[blk(fwd), blk(bwd), pl.BlockSpec(state_blk, lambda g: (seq_of(g)[2], 0, 0, 0, 0))],
        out_shape=[jax.ShapeDtypeStruct((n_tok, d), F32), jax.ShapeDtypeStruct((n_tok, d), F32),
                   jax.ShapeDtypeStruct((ctx_b,) + state_blk[1:], F32)],
        scratch_shapes=[pltpu.VMEM((2, HEADS, HEAD_DIM, HEAD_DIM), F32)],
        compiler_params=_params("arbitrary"),
        name="hgrn_scan",
    )(q, kf, bf, v, q, kb, bb, v, s0)


def _hgrn_out_kernel(of_ref, ob_ref, g_ref, x_ref, mod_ref, w_ref, o_ref):
    o = of_ref[...] + ob_ref[...]
    parts = []
    for h in range(HEADS):
        oh = o[:, h * HEAD_DIM:(h + 1) * HEAD_DIM]
        ms = jnp.mean(oh * oh, axis=-1, keepdims=True)
        parts.append(oh * lax.rsqrt(ms + EPS))
    on = jnp.concatenate(parts, axis=1) * g_ref[...]
    o_ref[...] = x_ref[...] + mod_ref[0][2:3] * _dot(on.astype(BF16), w_ref[...])


def _conv_in_kernel(x_ref, mod_ref, nw_ref, w_ref, u_ref):
    d = x_ref.shape[1]
    m = mod_ref[0]
    hn = _norm_mod(x_ref[...], nw_ref[...], m[0:1], m[1:2]).astype(BF16)
    p = _dot(hn, w_ref[...])
    u_ref[...] = p[:, :d] * jax.nn.sigmoid(p[:, d:])


def _conv_out_kernel(up_ref, u_ref, un_ref, x_ref, mod_ref, wdw_ref, bdw_ref, lng_ref, lnb_ref, w_ref,
                     o_ref, ext_ref, sh_ref, cv_ref, *, n_lat_tiles, tiles_per_lat_seq):
    i = pl.program_id(0)
    tm, d = u_ref.shape
    pos = i % tiles_per_lat_seq
    is_lat = i < n_lat_tiles
    first = jnp.logical_or(jnp.logical_not(is_lat), pos == 0)
    last = jnp.logical_or(jnp.logical_not(is_lat), pos == tiles_per_lat_seq - 1)
    ext_ref[0:HALO, :] = jnp.where(first, 0.0, up_ref[...])
    ext_ref[HALO:HALO + tm, :] = u_ref[...]
    ext_ref[HALO + tm:HALO + tm + HALO, :] = jnp.where(last, 0.0, un_ref[...])

    n_sh = sh_ref.shape[1]
    for s in range(1, SUBLANES):
        sh_ref[s - 1] = ext_ref[s:s + n_sh, :]

    rows, lanes = 16, 512
    off = HALO - CONV_K // 2

    def body(r, carry):
        base = pl.multiple_of(r * rows, rows)
        for lc in range(d // lanes):
            ls = slice(lc * lanes, (lc + 1) * lanes)
            acc = jnp.broadcast_to(bdw_ref[:, ls], (rows, lanes))
            for t in range(CONV_K):
                a, s = divmod(t + off, SUBLANES)
                start = pl.multiple_of(base + a * SUBLANES, SUBLANES)
                if s == 0:
                    tap = ext_ref[pl.ds(start, rows), ls]
                else:
                    tap = sh_ref[s - 1, pl.ds(start, rows), ls]
                acc = acc + wdw_ref[t:t + 1, ls] * tap
            cv_ref[pl.ds(base, rows), ls] = acc
        return carry

    lax.fori_loop(0, tm // rows, body, 0)

    u = cv_ref[...]
    mu = jnp.mean(u, axis=-1, keepdims=True)
    uc = u - mu
    var = jnp.mean(uc * uc, axis=-1, keepdims=True)
    y = uc * lax.rsqrt(var + EPS) * lng_ref[...] + lnb_ref[...]
    out = _dot(_silu(y).astype(BF16), w_ref[...])
    o_ref[...] = x_ref[...] + mod_ref[0][2:3] * out


def _grid_pos_embed(n_tok, dim):
    rows = n_tok // GRID_W
    rr, cc = jnp.meshgrid(jnp.arange(rows, dtype=F32), jnp.arange(GRID_W, dtype=F32), indexing='ij')
    nf = dim // 4
    omega = 1.0 / (10000.0 ** (jnp.arange(nf, dtype=F32) / nf))
    er = rr.reshape(-1)[:, None] * omega
    ec = cc.reshape(-1)[:, None] * omega
    return jnp.concatenate([jnp.sin(er), jnp.cos(er), jnp.sin(ec), jnp.cos(ec)], axis=-1)


def _lower_bounds(p):
    p = jax.nn.softmax(p.astype(F32), axis=0)
    return jnp.maximum(jnp.cumsum(p, axis=0) - p[0], 0.0)


def kernel(x_prompt, x_sample, c, state_hgrn, c_ctx, w_mod, b_mod, norm_mix, norm_mlp, hgrn_w_in, hgrn_lb_fwd, hgrn_lb_bwd, hgrn_g_norm, hgrn_w_out, conv_w_pw1, conv_w_dw, conv_b_dw, conv_ln_g, conv_ln_b, conv_w_pw2, mlp_w1, mlp_w2, final_norm):
    ctx_b, ctx_t, d = x_prompt.shape
    lat_b, lat_t, _ = x_sample.shape
    depth = w_mod.shape[0]
    d_ff = mlp_w1.shape[2]
    n_lat, n_ctx = lat_b * lat_t, ctx_b * ctx_t
    n_tok = n_lat + n_ctx
    assert d == HEADS * HEAD_DIM and lat_b < MOD_ROWS
    assert lat_t % TM == 0 and n_ctx % TM == 0 and ctx_t % CHUNK == 0 and ctx_t == TM_CONV and d_ff % TF == 0

    def mod_row(tm):
        return lambda i, *_: (jnp.where(i * tm < n_lat, (i * tm) // lat_t, lat_b), 0, 0)

    def mod_spec(tm):
        return pl.BlockSpec((1, N_MOD, d), mod_row(tm))

    def tok_spec(tm, width=d):
        return pl.BlockSpec((tm, width), lambda i, *_: (i, 0))

    def full_spec(shape):
        return pl.BlockSpec(shape, lambda i, *_: (0,) * len(shape))

    row = lambda a: a.reshape(1, -1)
    tok_f32 = jax.ShapeDtypeStruct((n_tok, d), F32)
    tok_bf16 = jax.ShapeDtypeStruct((n_tok, d), BF16)

    cvec = jnp.zeros((MOD_ROWS, d), F32).at[:lat_b].set(c).at[lat_b].set(c_ctx)
    mods = _modulation(cvec, w_mod, b_mod).reshape(depth, MOD_ROWS, N_MOD, d)

    lbs_f = _lower_bounds(hgrn_lb_fwd)
    lbs_b = _lower_bounds(hgrn_lb_bwd)

    pos = _grid_pos_embed(lat_t, d).astype(x_sample.dtype)
    x = jnp.concatenate([(x_sample + pos[None]).reshape(n_lat, d), x_prompt.reshape(n_ctx, d)], axis=0)

    new_states = []
    for i in range(depth):
        mod = mods[i]
        if i % 2 == 0:
            a = i // 2
            q, v, kf, bf, kb, bb, g = pl.pallas_call(
                _hgrn_in_kernel,
                grid=(n_tok // TM, 5),
                in_specs=[tok_spec(TM), mod_spec(TM), full_spec((1, d)),
                          pl.BlockSpec((d, d), lambda i, j: (0, j)),
                          full_spec((1, d)), full_spec((1, d)), full_spec((1, d))],
                out_specs=[tok_spec(TM)] * 7,
                out_shape=[tok_f32, tok_bf16, tok_f32, tok_f32, tok_f32, tok_f32, tok_f32],
                scratch_shapes=[pltpu.VMEM((TM, d), BF16)],
                compiler_params=_params("parallel", "arbitrary"),
                name="hgrn_in",
            )(x, mod, row(norm_mix[i]), hgrn_w_in[a].astype(BF16), row(lbs_f[a]), row(lbs_b[a]),
              row(hgrn_g_norm[a]))
            of, ob, st = _scan(q, kf, bf, kb, bb, v, state_hgrn, a, lat_b=lat_b, lat_t=lat_t,
                               ctx_b=ctx_b, ctx_t=ctx_t)
            new_states.append(st)
            x = pl.pallas_call(
                _hgrn_out_kernel,
                grid=(n_tok // TM,),
                in_specs=[tok_spec(TM), tok_spec(TM), tok_spec(TM), tok_spec(TM), mod_spec(TM),
                          full_spec((d, d))],
                out_specs=tok_spec(TM),
                out_shape=tok_f32,
                compiler_params=_params("parallel"),
                name="hgrn_out",
            )(of, ob, g, x, mod, hgrn_w_out[a].astype(BF16))
        else:
            b = i // 2
            u = pl.pallas_call(
                _conv_in_kernel,
                grid=(n_tok // TM,),
                in_specs=[tok_spec(TM), mod_spec(TM), full_spec((1, d)), full_spec((d, 2 * d))],
                out_specs=tok_spec(TM),
                out_shape=tok_f32,
                compiler_params=_params("parallel"),
                name="conv_in",
            )(x, mod, row(norm_mix[i]), conv_w_pw1[b].astype(BF16))
            tm = TM_CONV
            hb = tm // HALO
            n_halo = n_tok // HALO
            wdw = jnp.zeros((CONV_K + 1, d), F32).at[:CONV_K].set(conv_w_dw[b])
            x = pl.pallas_call(
                functools.partial(_conv_out_kernel, n_lat_tiles=n_lat // tm, tiles_per_lat_seq=lat_t // tm),
                grid=(n_tok // tm,),
                in_specs=[pl.BlockSpec((HALO, d), lambda i: (jnp.maximum(i * hb - 1, 0), 0)),
                          tok_spec(tm),
                          pl.BlockSpec((HALO, d), lambda i: (jnp.minimum((i + 1) * hb, n_halo - 1), 0)),
                          tok_spec(tm), mod_spec(tm), full_spec((CONV_K + 1, d)), full_spec((1, d)),
                          full_spec((1, d)), full_spec((1, d)), full_spec((d, d))],
                out_specs=tok_spec(tm),
                out_shape=tok_f32,
                scratch_shapes=[pltpu.VMEM((tm + 2 * HALO, d), F32),
                                pltpu.VMEM((SUBLANES - 1, tm + 2 * HALO - SUBLANES, d), F32),
                                pltpu.VMEM((tm, d), F32)],
                compiler_params=_params("parallel"),
                name="conv_out",
            )(u, u, u, x, mod, wdw, row(conv_b_dw[b]), row(conv_ln_g[b]), row(conv_ln_b[b]),
              conv_w_pw2[b].astype(BF16))

        x = pl.pallas_call(
            functools.partial(_mlp_kernel, final=(i == depth - 1)),
            grid=(n_tok // TM, d_ff // TF),
            in_specs=[tok_spec(TM), mod_spec(TM), full_spec((1, d)), full_spec((1, d)),
                      pl.BlockSpec((d, TF), lambda i, j: (0, j)),
                      pl.BlockSpec((TF, d), lambda i, j: (j, 0))],
            out_specs=tok_spec(TM),
            out_shape=tok_f32,
            scratch_shapes=[pltpu.VMEM((TM, d), BF16), pltpu.VMEM((TM, d), F32)],
            compiler_params=_params("parallel", "arbitrary"),
            name="mlp",
        )(x, mod, row(norm_mlp[i]), row(final_norm), mlp_w1[i].astype(BF16), mlp_w2[i].astype(BF16))

    y_sample = x[:n_lat].reshape(lat_b, lat_t, d)
    y_prompt = x[n_lat:].reshape(ctx_b, ctx_t, d)
    new_state_hgrn = jnp.stack(new_states, axis=1).astype(x_prompt.dtype)
    return (y_prompt, y_sample, new_state_hgrn)
```

```python
import functools

import jax
import jax.numpy as jnp
from jax import lax
from jax.experimental import pallas as pl
from jax.experimental.pallas import tpu as pltpu

F32 = jnp.float32
BF16 = jnp.bfloat16

N_MOD = 6
HEADS = 8
HEAD_DIM = 128
GRID_W = 64
CONV_K = 31
EPS = 1e-6
K_MAX = 1.0 - 1e-6

CHUNK = 128
SUB = 16
N_SUB = CHUNK // SUB
FAST_EXP_MAX = 75.0
SUBLANES = 8
HALO = 16
MOD_ROWS = 8

TM = 512
TM_CONV = 256
TF = 1024
VMEM_LIMIT = 56 * 1024 * 1024


def _params(*sem):
    return pltpu.CompilerParams(dimension_semantics=sem, vmem_limit_bytes=VMEM_LIMIT)


def _dot(a, b):
    return jnp.dot(a, b, preferred_element_type=F32)


def _dot_nt(a, b):
    return lax.dot_general(a, b, (((1,), (1,)), ((), ())), preferred_element_type=F32)


def _sigmoid(x):
    return 1.0 / (1.0 + jnp.exp(-x))


def _silu(x):
    return x * _sigmoid(x)


def _norm_mod(x, nw, shift, scale):
    ms = jnp.mean(x * x, axis=-1, keepdims=True)
    y = x * lax.rsqrt(ms + EPS) * nw
    return y * (1.0 + scale) + shift


def _mod_kernel(c_ref, w_ref, b_ref, o_ref):
    c = _silu(c_ref[...]).astype(BF16)
    o_ref[0] = _dot(c, w_ref[0].astype(BF16)) + b_ref[0]


def _modulation(cvec, w_mod, b_mod):
    depth, d, n = w_mod.shape
    tn = 1536
    return pl.pallas_call(
        _mod_kernel,
        grid=(depth, n // tn),
        in_specs=[
            pl.BlockSpec((MOD_ROWS, d), lambda l, j: (0, 0)),
            pl.BlockSpec((1, d, tn), lambda l, j: (l, 0, j)),
            pl.BlockSpec((1, 1, tn), lambda l, j: (l, 0, j)),
        ],
        out_specs=pl.BlockSpec((1, MOD_ROWS, tn), lambda l, j: (l, 0, j)),
        out_shape=jax.ShapeDtypeStruct((depth, MOD_ROWS, n), F32),
        compiler_params=_params("parallel", "parallel"),
        name="modulation",
    )(cvec, w_mod, b_mod.reshape(depth, 1, n))


def _mlp_kernel(x_ref, mod_ref, nw_ref, fw_ref, w1_ref, w2_ref, o_ref, hn_ref, acc_ref, *, final):
    j = pl.program_id(1)

    @pl.when(j == 0)
    def _():
        m = mod_ref[0]
        hn_ref[...] = _norm_mod(x_ref[...], nw_ref[...], m[3:4], m[4:5]).astype(BF16)
        acc_ref[...] = jnp.zeros_like(acc_ref)

    h = _dot(hn_ref[...], w1_ref[...])
    h = jnp.square(jnp.maximum(h, 0.0)).astype(BF16)
    acc_ref[...] += _dot(h, w2_ref[...])

    @pl.when(j == pl.num_programs(1) - 1)
    def _():
        y = x_ref[...] + mod_ref[0][5:6] * acc_ref[...]
        if final:
            ms = jnp.mean(y * y, axis=-1, keepdims=True)
            y = y * lax.rsqrt(ms + EPS) * fw_ref[...]
        o_ref[...] = y


def _hgrn_in_kernel(x_ref, mod_ref, nw_ref, w_ref, lbf_ref, lbb_ref, gn_ref,
                    q_ref, v_ref, kf_ref, bf_ref, kb_ref, bb_ref, g_ref, hn_ref):
    j = pl.program_id(1)
    tm = x_ref.shape[0]

    @pl.when(j == 0)
    def _():
        m = mod_ref[0]
        hn_ref[...] = _norm_mod(x_ref[...], nw_ref[...], m[0:1], m[1:2]).astype(BF16)

    def proj():
        return _dot(hn_ref[...], w_ref[...])

    def gates(lb_ref, k_ref, b_ref, reverse):
        k = jnp.minimum((1.0 - lb_ref[...]) * _sigmoid(-proj()), K_MAX)
        k_ref[...] = k.astype(BF16)
        lg = jnp.log(1.0 - k)
        row = lax.broadcasted_iota(jnp.int32, (CHUNK, CHUNK), 0)
        col = lax.broadcasted_iota(jnp.int32, (CHUNK, CHUNK), 1)
        tri = jnp.where((col >= row) if reverse else (col <= row), 1.0, 0.0).astype(BF16)
        for r in range(tm // CHUNK):
            x = lg[r * CHUNK:(r + 1) * CHUNK]
            hi = x.astype(BF16)
            lo = (x - hi.astype(F32)).astype(BF16)
            b_ref[r * CHUNK:(r + 1) * CHUNK, :] = _dot(tri, hi) + _dot(tri, lo)

    @pl.when(j == 0)
    def _():
        q_ref[...] = (_silu(proj()) * (HEAD_DIM ** -0.5)).astype(BF16)

    @pl.when(j == 1)
    def _():
        v_ref[...] = proj().astype(BF16)

    @pl.when(j == 2)
    def _():
        gates(lbf_ref, kf_ref, bf_ref, False)

    @pl.when(j == 3)
    def _():
        gates(lbb_ref, kb_ref, bb_ref, True)

    @pl.when(j == 4)
    def _():
        g_ref[...] = (gn_ref[...] * _silu(proj())).astype(BF16)


def _scan_scores(q, k, b, reverse):
    q = q.astype(F32)
    k = k.astype(F32)
    row = lax.broadcasted_iota(jnp.int32, (CHUNK, CHUNK), 0)
    col = lax.broadcasted_iota(jnp.int32, (CHUNK, CHUNK), 1)
    keep = (col >= row) if reverse else (col <= row)
    blocks = []
    for i in range(N_SUB):
        lo, hi = i * SUB, (i + 1) * SUB
        if reverse:
            ref = b[hi:hi + 1] if i < N_SUB - 1 else jnp.zeros((1, HEAD_DIM), F32)
            s0, s1 = lo, CHUNK
        else:
            ref = b[lo - 1:lo] if i > 0 else jnp.zeros((1, HEAD_DIM), F32)
            s0, s1 = 0, hi
        qi = (q[lo:hi] * jnp.exp(b[lo:hi] - ref)).astype(BF16)
        ks = (k[s0:s1] * jnp.exp(ref - b[s0:s1])).astype(BF16)
        parts = []
        if s0 > 0:
            parts.append(jnp.zeros((s0, HEAD_DIM), BF16))
        parts.append(ks)
        if s1 < CHUNK:
            parts.append(jnp.zeros((CHUNK - s1, HEAD_DIM), BF16))
        ks = jnp.concatenate(parts, axis=0) if len(parts) > 1 else ks
        blocks.append(_dot_nt(qi, ks))
    return jnp.where(keep, jnp.concatenate(blocks, axis=0), 0.0)


def _max_block_exponent(b_ref, reverse):
    edge = 0 if reverse else SUB - 1
    far = [b_ref[i * SUB + edge:i * SUB + edge + 1, :] for i in range(N_SUB)]
    if reverse:
        worst = -far[N_SUB - 1]
        for i in range(N_SUB - 1):
            worst = jnp.maximum(worst, far[i + 1] - far[i])
    else:
        worst = -far[0]
        for i in range(1, N_SUB):
            worst = jnp.maximum(worst, far[i - 1] - far[i])
    return jnp.max(worst)


def _exact_diag_blocks(q_ref, k_ref, b_ref, a_ref, reverse):
    lane = lax.broadcasted_iota(jnp.int32, (SUB, HEAD_DIM), 1)
    rowi = lax.broadcasted_iota(jnp.int32, (SUB, HEAD_DIM), 0)

    def block(idx, carry):
        h = idx // N_SUB
        r0 = pl.multiple_of((idx % N_SUB) * SUB, SUB)
        lanes = pl.ds(pl.multiple_of(h * HEAD_DIM, HEAD_DIM), HEAD_DIM)
        qi = q_ref[pl.ds(r0, SUB), lanes].astype(F32)
        ki = k_ref[pl.ds(r0, SUB), lanes].astype(F32)
        bi = b_ref[pl.ds(r0, SUB), lanes]

        def column(s, acc):
            pick = rowi == s
            ks = jnp.sum(jnp.where(pick, ki, 0.0), axis=0, keepdims=True)
            bs = jnp.sum(jnp.where(pick, bi, 0.0), axis=0, keepdims=True)
            w = jnp.exp(jnp.minimum(bi - bs, 0.0))
            c = jnp.sum(qi * ks * w, axis=-1, keepdims=True)
            return jnp.where(lane == r0 + s, c, acc)

        acc = lax.fori_loop(0, SUB, column, jnp.zeros((SUB, HEAD_DIM), F32))
        t = rowi + r0
        keep = (lane >= t) if reverse else (lane <= t)
        in_block = jnp.logical_and(lane >= r0, lane < r0 + SUB)
        old = a_ref[h, pl.ds(r0, SUB), :]
        a_ref[h, pl.ds(r0, SUB), :] = jnp.where(in_block, jnp.where(keep, acc, 0.0), old)
        return carry

    lax.fori_loop(0, HEADS * N_SUB, block, 0)


def _scan_apply(q, k, b, v, a, st_ref, reverse):
    b_last = b[0:1] if reverse else b[CHUNK - 1:CHUNK]
    st = st_ref[...]
    vt = v.T
    lhs = jnp.concatenate([(q.astype(F32) * jnp.exp(b)).astype(BF16), a.astype(BF16)], axis=1)
    rhs = jnp.concatenate([st.astype(BF16), vt], axis=1)
    k_end = (k.astype(F32) * jnp.exp(b_last - b)).astype(BF16)
    st_ref[...] = st * jnp.exp(b_last) + _dot(vt, k_end)
    return _dot_nt(lhs, rhs)


def _scan_kernel(qf_ref, kf_ref, bf_ref, vf_ref, qb_ref, kb_ref, bb_ref, vb_ref, s0_ref,
                 of_ref, ob_ref, sout_ref, st_ref, a_ref, *, lat_chunks, lat_nc, ctx_nc):
    g = pl.program_id(0)
    is_lat = g < lat_chunks
    c = jnp.where(is_lat, g % lat_nc, (g - lat_chunks) % ctx_nc)
    last = c == jnp.where(is_lat, lat_nc, ctx_nc) - 1

    @pl.when(jnp.logical_and(c == 0, is_lat))
    def _():
        for d in range(2):
            for h in range(HEADS):
                st_ref[d, h] = s0_ref[0, 0, d, h].T

    @pl.when(jnp.logical_and(c == 0, jnp.logical_not(is_lat)))
    def _():
        st_ref[...] = jnp.zeros_like(st_ref)

    dirs = ((qf_ref, kf_ref, bf_ref, vf_ref, of_ref, False), (qb_ref, kb_ref, bb_ref, vb_ref, ob_ref, True))
    head = lambda h: slice(h * HEAD_DIM, (h + 1) * HEAD_DIM)

    for h in range(HEADS):
        for d, (q_ref, k_ref, b_ref, _, _, reverse) in enumerate(dirs):
            a_ref[d, h] = _scan_scores(q_ref[:, head(h)], k_ref[:, head(h)], b_ref[:, head(h)], reverse)

    worst = jnp.maximum(_max_block_exponent(bf_ref, False), _max_block_exponent(bb_ref, True))

    @pl.when(worst > FAST_EXP_MAX)
    def _():
        for d, (q_ref, k_ref, b_ref, _, _, reverse) in enumerate(dirs):
            _exact_diag_blocks(q_ref, k_ref, b_ref, a_ref.at[d], reverse)

    for h in range(HEADS):
        for d, (q_ref, k_ref, b_ref, v_ref, o_ref, reverse) in enumerate(dirs):
            o_ref[:, head(h)] = _scan_apply(q_ref[:, head(h)], k_ref[:, head(h)], b_ref[:, head(h)],
                                            v_ref[:, head(h)], a_ref[d, h], st_ref.at[d, h], reverse)

    @pl.when(jnp.logical_and(last, jnp.logical_not(is_lat)))
    def _():
        for d in range(2):
            for h in range(HEADS):
                sout_ref[0, d, h] = st_ref[d, h].T


def _scan(q, kf, bf, kb, bb, v, s0, layer, *, lat_b, lat_t, ctx_b, ctx_t):
    n_tok, d = q.shape
    lat_nc, ctx_nc = lat_t // CHUNK, ctx_t // CHUNK
    lat_chunks = lat_b * lat_nc

    def seq_of(g):
        is_lat = g < lat_chunks
        cs = (g - lat_chunks) // ctx_nc
        start = jnp.where(is_lat, (g // lat_nc) * lat_nc, lat_chunks + cs * ctx_nc)
        return start, jnp.where(is_lat, lat_nc, ctx_nc), jnp.where(is_lat, 0, cs)

    def fwd(g):
        return (g, 0)

    def bwd(g):
        start, nc, _ = seq_of(g)
        return (2 * start + nc - 1 - g, 0)

    blk = lambda im: pl.BlockSpec((CHUNK, d), im)
    state_blk = (1, 2, HEADS, HEAD_DIM, HEAD_DIM)
    return pl.pallas_call(
        functools.partial(_scan_kernel, lat_chunks=lat_chunks, lat_nc=lat_nc, ctx_nc=ctx_nc),
        grid=(n_tok // CHUNK,),
        in_specs=[blk(fwd), blk(fwd), blk(fwd), blk(fwd), blk(bwd), blk(bwd), blk(bwd), blk(bwd),
                  pl.BlockSpec((1, 1) + state_blk[1:],
                               lambda g: (jnp.minimum(g // lat_nc, lat_b - 1), layer, 0, 0, 0, 0))],
        out_specs=[blk(fwd), blk(bwd), pl.BlockSpec(state_blk, lambda g: (seq_of(g)[2], 0, 0, 0, 0))],
        out_shape=[jax.ShapeDtypeStruct((n_tok, d), F32), jax.ShapeDtypeStruct((n_tok, d), F32),
                   jax.ShapeDtypeStruct((ctx_b,) + state_blk[1:], F32)],
        scratch_shapes=[pltpu.VMEM((2, HEADS, HEAD_DIM, HEAD_DIM), F32),
                        pltpu.VMEM((2, HEADS, CHUNK, CHUNK), F32)],
        compiler_params=_params("arbitrary"),
        name="hgrn_scan",
    )(q, kf, bf, v, q, kb, bb, v, s0)


def _hgrn_out_kernel(of_ref, ob_ref, g_ref, x_ref, mod_ref, w_ref, o_ref):
    o = of_ref[...] + ob_ref[...]
    parts = []
    for h in range(HEADS):
        oh = o[:, h * HEAD_DIM:(h + 1) * HEAD_DIM]
        ms = jnp.mean(oh * oh, axis=-1, keepdims=True)
        parts.append(oh * lax.rsqrt(ms + EPS))
    on = jnp.concatenate(parts, axis=1) * g_ref[...].astype(F32)
    o_ref[...] = x_ref[...] + mod_ref[0][2:3] * _dot(on.astype(BF16), w_ref[...])


def _conv_in_kernel(x_ref, mod_ref, nw_ref, w_ref, u_ref):
    d = x_ref.shape[1]
    m = mod_ref[0]
    hn = _norm_mod(x_ref[...], nw_ref[...], m[0:1], m[1:2]).astype(BF16)
    p = _dot(hn, w_ref[...])
    u_ref[...] = p[:, :d] * _sigmoid(p[:, d:])


def _conv_out_kernel(up_ref, u_ref, un_ref, x_ref, mod_ref, wdw_ref, bdw_ref, lng_ref, lnb_ref, w_ref,
                     o_ref, ext_ref, sh_ref, cv_ref, *, n_lat_tiles, tiles_per_lat_seq):
    i = pl.program_id(0)
    tm, d = u_ref.shape
    pos = i % tiles_per_lat_seq
    is_lat = i < n_lat_tiles
    first = jnp.logical_or(jnp.logical_not(is_lat), pos == 0)
    last = jnp.logical_or(jnp.logical_not(is_lat), pos == tiles_per_lat_seq - 1)
    ext_ref[0:HALO, :] = jnp.where(first, 0.0, up_ref[...])
    ext_ref[HALO:HALO + tm, :] = u_ref[...]
    ext_ref[HALO + tm:HALO + tm + HALO, :] = jnp.where(last, 0.0, un_ref[...])

    n_sh = sh_ref.shape[1]
    for s in range(1, SUBLANES):
        sh_ref[s - 1] = ext_ref[s:s + n_sh, :]

    rows, lanes = 16, 512
    off = HALO - CONV_K // 2

    def body(r, carry):
        base = pl.multiple_of(r * rows, rows)
        for lc in range(d // lanes):
            ls = slice(lc * lanes, (lc + 1) * lanes)
            acc = jnp.broadcast_to(bdw_ref[:, ls], (rows, lanes))
            for t in range(CONV_K):
                a, s = divmod(t + off, SUBLANES)
                start = pl.multiple_of(base + a * SUBLANES, SUBLANES)
                if s == 0:
                    tap = ext_ref[pl.ds(start, rows), ls]
                else:
                    tap = sh_ref[s - 1, pl.ds(start, rows), ls]
                acc = acc + wdw_ref[t:t + 1, ls] * tap
            cv_ref[pl.ds(base, rows), ls] = acc
        return carry

    lax.fori_loop(0, tm // rows, body, 0)

    u = cv_ref[...]
    mu = jnp.mean(u, axis=-1, keepdims=True)
    uc = u - mu
    var = jnp.mean(uc * uc, axis=-1, keepdims=True)
    y = uc * lax.rsqrt(var + EPS) * lng_ref[...] + lnb_ref[...]
    out = _dot(_silu(y).astype(BF16), w_ref[...])
    o_ref[...] = x_ref[...] + mod_ref[0][2:3] * out


def _grid_pos_embed(n_tok, dim):
    rows = n_tok // GRID_W
    rr, cc = jnp.meshgrid(jnp.arange(rows, dtype=F32), jnp.arange(GRID_W, dtype=F32), indexing='ij')
    nf = dim // 4
    omega = 1.0 / (10000.0 ** (jnp.arange(nf, dtype=F32) / nf))
    er = rr.reshape(-1)[:, None] * omega
    ec = cc.reshape(-1)[:, None] * omega
    return jnp.concatenate([jnp.sin(er), jnp.cos(er), jnp.sin(ec), jnp.cos(ec)], axis=-1)


def _lower_bounds(p):
    p = jax.nn.softmax(p.astype(F32), axis=0)
    return jnp.maximum(jnp.cumsum(p, axis=0) - p[0], 0.0)


def kernel(x_prompt, x_sample, c, state_hgrn, c_ctx, w_mod, b_mod, norm_mix, norm_mlp, hgrn_w_in, hgrn_lb_fwd, hgrn_lb_bwd, hgrn_g_norm, hgrn_w_out, conv_w_pw1, conv_w_dw, conv_b_dw, conv_ln_g, conv_ln_b, conv_w_pw2, mlp_w1, mlp_w2, final_norm):
    ctx_b, ctx_t, d = x_prompt.shape
    lat_b, lat_t, _ = x_sample.shape
    depth = w_mod.shape[0]
    d_ff = mlp_w1.shape[2]
    n_lat, n_ctx = lat_b * lat_t, ctx_b * ctx_t
    n_tok = n_lat + n_ctx
    assert d == HEADS * HEAD_DIM and lat_b < MOD_ROWS
    assert lat_t % TM == 0 and n_ctx % TM == 0 and ctx_t % CHUNK == 0 and ctx_t == TM_CONV and d_ff % TF == 0

    def mod_row(tm):
        return lambda i, *_: (jnp.where(i * tm < n_lat, (i * tm) // lat_t, lat_b), 0, 0)

    def mod_spec(tm):
        return pl.BlockSpec((1, N_MOD, d), mod_row(tm))

    def tok_spec(tm, width=d):
        return pl.BlockSpec((tm, width), lambda i, *_: (i, 0))

    def full_spec(shape):
        return pl.BlockSpec(shape, lambda i, *_: (0,) * len(shape))

    row = lambda a: a.reshape(1, -1)
    tok_f32 = jax.ShapeDtypeStruct((n_tok, d), F32)
    tok_bf16 = jax.ShapeDtypeStruct((n_tok, d), BF16)

    cvec = jnp.zeros((MOD_ROWS, d), F32).at[:lat_b].set(c).at[lat_b].set(c_ctx)
    mods = _modulation(cvec, w_mod, b_mod).reshape(depth, MOD_ROWS, N_MOD, d)

    lbs_f = _lower_bounds(hgrn_lb_fwd)
    lbs_b = _lower_bounds(hgrn_lb_bwd)

    pos = _grid_pos_embed(lat_t, d).astype(x_sample.dtype)
    x = jnp.concatenate([(x_sample + pos[None]).reshape(n_lat, d), x_prompt.reshape(n_ctx, d)], axis=0)

    new_states = []
    for i in range(depth):
        mod = mods[i]
        if i % 2 == 0:
            a = i // 2
            q, v, kf, bf, kb, bb, g = pl.pallas_call(
                _hgrn_in_kernel,
                grid=(n_tok // TM, 5),
                in_specs=[tok_spec(TM), mod_spec(TM), full_spec((1, d)),
                          pl.BlockSpec((d, d), lambda i, j: (0, j)),
                          full_spec((1, d)), full_spec((1, d)), full_spec((1, d))],
                out_specs=[tok_spec(TM)] * 7,
                out_shape=[tok_bf16, tok_bf16, tok_bf16, tok_f32, tok_bf16, tok_f32, tok_bf16],
                scratch_shapes=[pltpu.VMEM((TM, d), BF16)],
                compiler_params=_params("parallel", "arbitrary"),
                name="hgrn_in",
            )(x, mod, row(norm_mix[i]), hgrn_w_in[a].astype(BF16), row(lbs_f[a]), row(lbs_b[a]),
              row(hgrn_g_norm[a]))
            of, ob, st = _scan(q, kf, bf, kb, bb, v, state_hgrn, a, lat_b=lat_b, lat_t=lat_t,
                               ctx_b=ctx_b, ctx_t=ctx_t)
            new_states.append(st)
            x = pl.pallas_call(
                _hgrn_out_kernel,
                grid=(n_tok // TM,),
                in_specs=[tok_spec(TM), tok_spec(TM), tok_spec(TM), tok_spec(TM), mod_spec(TM),
                          full_spec((d, d))],
                out_specs=tok_spec(TM),
                out_shape=tok_f32,
                compiler_params=_params("parallel"),
                name="hgrn_out",
            )(of, ob, g, x, mod, hgrn_w_out[a].astype(BF16))
        else:
            b = i // 2
            u = pl.pallas_call(
                _conv_in_kernel,
                grid=(n_tok // TM,),
                in_specs=[tok_spec(TM), mod_spec(TM), full_spec((1, d)), full_spec((d, 2 * d))],
                out_specs=tok_spec(TM),
                out_shape=tok_f32,
                compiler_params=_params("parallel"),
                name="conv_in",
            )(x, mod, row(norm_mix[i]), conv_w_pw1[b].astype(BF16))
            tm = TM_CONV
            hb = tm // HALO
            n_halo = n_tok // HALO
            wdw = jnp.zeros((CONV_K + 1, d), F32).at[:CONV_K].set(conv_w_dw[b])
            x = pl.pallas_call(
                functools.partial(_conv_out_kernel, n_lat_tiles=n_lat // tm, tiles_per_lat_seq=lat_t // tm),
                grid=(n_tok // tm,),
                in_specs=[pl.BlockSpec((HALO, d), lambda i: (jnp.maximum(i * hb - 1, 0), 0)),
                          tok_spec(tm),
                          pl.BlockSpec((HALO, d), lambda i: (jnp.minimum((i + 1) * hb, n_halo - 1), 0)),
                          tok_spec(tm), mod_spec(tm), full_spec((CONV_K + 1, d)), full_spec((1, d)),
                          full_spec((1, d)), full_spec((1, d)), full_spec((d, d))],
                out_specs=tok_spec(tm),
                out_shape=tok_f32,
                scratch_shapes=[pltpu.VMEM((tm + 2 * HALO, d), F32),
                                pltpu.VMEM((SUBLANES - 1, tm + 2 * HALO - SUBLANES, d), F32),
                                pltpu.VMEM((tm, d), F32)],
                compiler_params=_params("parallel"),
                name="conv_out",
            )(u, u, u, x, mod, wdw, row(conv_b_dw[b]), row(conv_ln_g[b]), row(conv_ln_b[b]),
              conv_w_pw2[b].astype(BF16))

        x = pl.pallas_call(
            functools.partial(_mlp_kernel, final=(i == depth - 1)),
            grid=(n_tok // TM, d_ff // TF),
            in_specs=[tok_spec(TM), mod_spec(TM), full_spec((1, d)), full_spec((1, d)),
                      pl.BlockSpec((d, TF), lambda i, j: (0, j)),
                      pl.BlockSpec((TF, d), lambda i, j: (j, 0))],
            out_specs=tok_spec(TM),
            out_shape=tok_f32,
            scratch_shapes=[pltpu.VMEM((TM, d), BF16), pltpu.VMEM((TM, d), F32)],
            compiler_params=_params("parallel", "arbitrary"),
            name="mlp",
        )(x, mod, row(norm_mlp[i]), row(final_norm), mlp_w1[i].astype(BF16), mlp_w2[i].astype(BF16))

    y_sample = x[:n_lat].reshape(lat_b, lat_t, d)
    y_prompt = x[n_lat:].reshape(ctx_b, ctx_t, d)
    new_state_hgrn = jnp.stack(new_states, axis=1).astype(x_prompt.dtype)
    return (y_prompt, y_sample, new_state_hgrn)
```

```python
import functools

import jax
import jax.numpy as jnp
from jax import lax
from jax.experimental import pallas as pl
from jax.experimental.pallas import tpu as pltpu

F32 = jnp.float32
BF16 = jnp.bfloat16

N_MOD = 6
HEADS = 8
HEAD_DIM = 128
GRID_W = 64
CONV_K = 31
EPS = 1e-6
K_MAX = 1.0 - 1e-6

CHUNK = 128
SUB = 16
N_SUB = CHUNK // SUB
FAST_EXP_MAX = 75.0
SUBLANES = 8
HALO = 16
MOD_ROWS = 8

TM = 512
TM_CONV = 256
TF = 1024
VMEM_LIMIT = 56 * 1024 * 1024


def _params(*sem):
    return pltpu.CompilerParams(dimension_semantics=sem, vmem_limit_bytes=VMEM_LIMIT)


def _dot(a, b):
    return jnp.dot(a, b, preferred_element_type=F32)


def _dot_nt(a, b):
    return lax.dot_general(a, b, (((1,), (1,)), ((), ())), preferred_element_type=F32)


def _sigmoid(x):
    return 1.0 / (1.0 + jnp.exp(-x))


def _silu(x):
    return x * _sigmoid(x)


def _norm_mod(x, nw, shift, scale):
    ms = jnp.mean(x * x, axis=-1, keepdims=True)
    y = x * lax.rsqrt(ms + EPS) * nw
    return y * (1.0 + scale) + shift


def _mod_kernel(c_ref, w_ref, b_ref, o_ref):
    c = _silu(c_ref[...]).astype(BF16)
    o_ref[0] = _dot(c, w_ref[0].astype(BF16)) + b_ref[0]


def _modulation(cvec, w_mod, b_mod):
    depth, d, n = w_mod.shape
    tn = 1536
    return pl.pallas_call(
        _mod_kernel,
        grid=(depth, n // tn),
        in_specs=[
            pl.BlockSpec((MOD_ROWS, d), lambda l, j: (0, 0)),
            pl.BlockSpec((1, d, tn), lambda l, j: (l, 0, j)),
            pl.BlockSpec((1, 1, tn), lambda l, j: (l, 0, j)),
        ],
        out_specs=pl.BlockSpec((1, MOD_ROWS, tn), lambda l, j: (l, 0, j)),
        out_shape=jax.ShapeDtypeStruct((depth, MOD_ROWS, n), F32),
        compiler_params=_params("parallel", "parallel"),
        name="modulation",
    )(cvec, w_mod, b_mod.reshape(depth, 1, n))


def _mlp_kernel(x_ref, mod_ref, nw_ref, fw_ref, w1_ref, w2_ref, *o_refs, final, n_lat_tiles):
    m = mod_ref[0]
    x = x_ref[...]
    hn = _norm_mod(x, nw_ref[...], m[3:4], m[4:5]).astype(BF16)
    acc = None
    for j in range(w1_ref.shape[1] // TF):
        h = _dot(hn, w1_ref[:, j * TF:(j + 1) * TF])
        h = jnp.square(jnp.maximum(h, 0.0)).astype(BF16)
        part = _dot(h, w2_ref[j * TF:(j + 1) * TF, :])
        acc = part if acc is None else acc + part
    y = x + m[5:6] * acc
    if not final:
        o_refs[0][...] = y
        return
    ms = jnp.mean(y * y, axis=-1, keepdims=True)
    y = y * lax.rsqrt(ms + EPS) * fw_ref[...]
    lat_ref, ctx_ref = o_refs
    is_lat = pl.program_id(0) < n_lat_tiles

    @pl.when(is_lat)
    def _():
        lat_ref[...] = y

    @pl.when(jnp.logical_not(is_lat))
    def _():
        ctx_ref[...] = y


def _hgrn_in_kernel(*refs, first, n_lat_tiles):
    if first:
        (xs_ref, pos_ref, xp_ref, mod_ref, nw_ref, w_ref, lbf_ref, lbb_ref, gn_ref,
         q_ref, v_ref, kf_ref, bf_ref, kb_ref, bb_ref, g_ref, x0_ref) = refs
        is_lat = pl.program_id(0) < n_lat_tiles
        x = jnp.where(is_lat, xs_ref[...] + pos_ref[...], xp_ref[...])
        x0_ref[...] = x
    else:
        (x_ref, mod_ref, nw_ref, w_ref, lbf_ref, lbb_ref, gn_ref,
         q_ref, v_ref, kf_ref, bf_ref, kb_ref, bb_ref, g_ref) = refs
        x = x_ref[...]
    tm, d = x.shape
    m = mod_ref[0]
    hn = _norm_mod(x, nw_ref[...], m[0:1], m[1:2]).astype(BF16)

    def proj(j):
        return _dot(hn, w_ref[:, j * d:(j + 1) * d])

    def gates(j, lb_ref, k_ref, b_ref, reverse):
        k = jnp.minimum((1.0 - lb_ref[...]) * _sigmoid(-proj(j)), K_MAX)
        k_ref[...] = k.astype(BF16)
        lg = jnp.log(1.0 - k)
        row = lax.broadcasted_iota(jnp.int32, (CHUNK, CHUNK), 0)
        col = lax.broadcasted_iota(jnp.int32, (CHUNK, CHUNK), 1)
        tri = jnp.where((col >= row) if reverse else (col <= row), 1.0, 0.0).astype(BF16)
        for r in range(tm // CHUNK):
            x = lg[r * CHUNK:(r + 1) * CHUNK]
            hi = x.astype(BF16)
            lo = (x - hi.astype(F32)).astype(BF16)
            b_ref[r * CHUNK:(r + 1) * CHUNK, :] = _dot(tri, hi) + _dot(tri, lo)

    q_ref[...] = (_silu(proj(0)) * (HEAD_DIM ** -0.5)).astype(BF16)
    v_ref[...] = proj(1).astype(BF16)
    gates(2, lbf_ref, kf_ref, bf_ref, False)
    gates(3, lbb_ref, kb_ref, bb_ref, True)
    g_ref[...] = (gn_ref[...] * _silu(proj(4))).astype(BF16)


def _scan_scores(q, k, b, reverse):
    q = q.astype(F32)
    k = k.astype(F32)
    row = lax.broadcasted_iota(jnp.int32, (CHUNK, CHUNK), 0)
    col = lax.broadcasted_iota(jnp.int32, (CHUNK, CHUNK), 1)
    keep = (col >= row) if reverse else (col <= row)
    blocks = []
    for i in range(N_SUB):
        lo, hi = i * SUB, (i + 1) * SUB
        if reverse:
            ref = b[hi:hi + 1] if i < N_SUB - 1 else jnp.zeros((1, HEAD_DIM), F32)
            s0, s1 = lo, CHUNK
        else:
            ref = b[lo - 1:lo] if i > 0 else jnp.zeros((1, HEAD_DIM), F32)
            s0, s1 = 0, hi
        qi = (q[lo:hi] * jnp.exp(b[lo:hi] - ref)).astype(BF16)
        ks = (k[s0:s1] * jnp.exp(ref - b[s0:s1])).astype(BF16)
        parts = []
        if s0 > 0:
            parts.append(jnp.zeros((s0, HEAD_DIM), BF16))
        parts.append(ks)
        if s1 < CHUNK:
            parts.append(jnp.zeros((CHUNK - s1, HEAD_DIM), BF16))
        ks = jnp.concatenate(parts, axis=0) if len(parts) > 1 else ks
        blocks.append(_dot_nt(qi, ks))
    return jnp.where(keep, jnp.concatenate(blocks, axis=0), 0.0)


def _max_block_exponent(b_ref, reverse):
    edge = 0 if reverse else SUB - 1
    far = [b_ref[i * SUB + edge:i * SUB + edge + 1, :] for i in range(N_SUB)]
    if reverse:
        worst = -far[N_SUB - 1]
        for i in range(N_SUB - 1):
            worst = jnp.maximum(worst, far[i + 1] - far[i])
    else:
        worst = -far[0]
        for i in range(1, N_SUB):
            worst = jnp.maximum(worst, far[i - 1] - far[i])
    return jnp.max(worst)


def _exact_diag_blocks(q_ref, k_ref, b_ref, a_ref, reverse):
    lane = lax.broadcasted_iota(jnp.int32, (SUB, HEAD_DIM), 1)
    rowi = lax.broadcasted_iota(jnp.int32, (SUB, HEAD_DIM), 0)

    def block(idx, carry):
        h = idx // N_SUB
        r0 = pl.multiple_of((idx % N_SUB) * SUB, SUB)
        lanes = pl.ds(pl.multiple_of(h * HEAD_DIM, HEAD_DIM), HEAD_DIM)
        qi = q_ref[pl.ds(r0, SUB), lanes].astype(F32)
        ki = k_ref[pl.ds(r0, SUB), lanes].astype(F32)
        bi = b_ref[pl.ds(r0, SUB), lanes]

        def column(s, acc):
            pick = rowi == s
            ks = jnp.sum(jnp.where(pick, ki, 0.0), axis=0, keepdims=True)
            bs = jnp.sum(jnp.where(pick, bi, 0.0), axis=0, keepdims=True)
            w = jnp.exp(jnp.minimum(bi - bs, 0.0))
            c = jnp.sum(qi * ks * w, axis=-1, keepdims=True)
            return jnp.where(lane == r0 + s, c, acc)

        acc = lax.fori_loop(0, SUB, column, jnp.zeros((SUB, HEAD_DIM), F32))
        t = rowi + r0
        keep = (lane >= t) if reverse else (lane <= t)
        in_block = jnp.logical_and(lane >= r0, lane < r0 + SUB)
        old = a_ref[h, pl.ds(r0, SUB), :]
        a_ref[h, pl.ds(r0, SUB), :] = jnp.where(in_block, jnp.where(keep, acc, 0.0), old)
        return carry

    lax.fori_loop(0, HEADS * N_SUB, block, 0)


def _scan_apply(q, k, b, v, a, st_ref, reverse):
    b_last = b[0:1] if reverse else b[CHUNK - 1:CHUNK]
    st = st_ref[...]
    vt = v.T
    lhs = jnp.concatenate([(q.astype(F32) * jnp.exp(b)).astype(BF16), a.astype(BF16)], axis=1)
    rhs = jnp.concatenate([st.astype(BF16), vt], axis=1)
    k_end = (k.astype(F32) * jnp.exp(b_last - b)).astype(BF16)
    st_ref[...] = st * jnp.exp(b_last) + _dot(vt, k_end)
    return _dot_nt(lhs, rhs)


def _scan_kernel(qf_ref, kf_ref, bf_ref, vf_ref, qb_ref, kb_ref, bb_ref, vb_ref, s0_ref,
                 of_ref, ob_ref, sout_ref, st_ref, a_ref, *, lat_chunks, lat_nc, ctx_nc):
    g = pl.program_id(0)
    is_lat = g < lat_chunks
    c = jnp.where(is_lat, g % lat_nc, (g - lat_chunks) % ctx_nc)
    last = c == jnp.where(is_lat, lat_nc, ctx_nc) - 1

    @pl.when(jnp.logical_and(c == 0, is_lat))
    def _():
        for d in range(2):
            for h in range(HEADS):
                st_ref[d, h] = s0_ref[0, 0, d, h].T

    @pl.when(jnp.logical_and(c == 0, jnp.logical_not(is_lat)))
    def _():
        st_ref[...] = jnp.zeros_like(st_ref)

    dirs = ((qf_ref, kf_ref, bf_ref, vf_ref, of_ref, False), (qb_ref, kb_ref, bb_ref, vb_ref, ob_ref, True))
    head = lambda h: slice(h * HEAD_DIM, (h + 1) * HEAD_DIM)

    for h in range(HEADS):
        for d, (q_ref, k_ref, b_ref, _, _, reverse) in enumerate(dirs):
            a_ref[d, h] = _scan_scores(q_ref[:, head(h)], k_ref[:, head(h)], b_ref[:, head(h)], reverse)

    worst = jnp.maximum(_max_block_exponent(bf_ref, False), _max_block_exponent(bb_ref, True))

    @pl.when(worst > FAST_EXP_MAX)
    def _():
        for d, (q_ref, k_ref, b_ref, _, _, reverse) in enumerate(dirs):
            _exact_diag_blocks(q_ref, k_ref, b_ref, a_ref.at[d], reverse)

    for h in range(HEADS):
        for d, (q_ref, k_ref, b_ref, v_ref, o_ref, reverse) in enumerate(dirs):
            o_ref[:, head(h)] = _scan_apply(q_ref[:, head(h)], k_ref[:, head(h)], b_ref[:, head(h)],
                                            v_ref[:, head(h)], a_ref[d, h], st_ref.at[d, h], reverse)

    @pl.when(jnp.logical_and(last, jnp.logical_not(is_lat)))
    def _():
        for d in range(2):
            for h in range(HEADS):
                sout_ref[0, d, h] = st_ref[d, h].T


def _scan(q, kf, bf, kb, bb, v, s0, layer, *, lat_b, lat_t, ctx_b, ctx_t):
    n_tok, d = q.shape
    lat_nc, ctx_nc = lat_t // CHUNK, ctx_t // CHUNK
    lat_chunks = lat_b * lat_nc

    def seq_of(g):
        is_lat = g < lat_chunks
        cs = (g - lat_chunks) // ctx_nc
        start = jnp.where(is_lat, (g // lat_nc) * lat_nc, lat_chunks + cs * ctx_nc)
        return start, jnp.where(is_lat, lat_nc, ctx_nc), jnp.where(is_lat, 0, cs)

    def fwd(g):
        return (g, 0)

    def bwd(g):
        start, nc, _ = seq_of(g)
        return (2 * start + nc - 1 - g, 0)

    blk = lambda im: pl.BlockSpec((CHUNK, d), im)
    state_blk = (1, 2, HEADS, HEAD_DIM, HEAD_DIM)
    return pl.pallas_call(
        functools.partial(_scan_kernel, lat_chunks=lat_chunks, lat_nc=lat_nc, ctx_nc=ctx_nc),
        grid=(n_tok // CHUNK,),
        in_specs=[blk(fwd), blk(fwd), blk(fwd), blk(fwd), blk(bwd), blk(bwd), blk(bwd), blk(bwd),
                  pl.BlockSpec((1, 1) + state_blk[1:],
                               lambda g: (jnp.minimum(g // lat_nc, lat_b - 1), layer, 0, 0, 0, 0))],
        out_specs=[blk(fwd), blk(bwd), pl.BlockSpec(state_blk, lambda g: (seq_of(g)[2], 0, 0, 0, 0))],
        out_shape=[jax.ShapeDtypeStruct((n_tok, d), F32), jax.ShapeDtypeStruct((n_tok, d), F32),
                   jax.ShapeDtypeStruct((ctx_b,) + state_blk[1:], F32)],
        scratch_shapes=[pltpu.VMEM((2, HEADS, HEAD_DIM, HEAD_DIM), F32),
                        pltpu.VMEM((2, HEADS, CHUNK, CHUNK), F32)],
        compiler_params=_params("arbitrary"),
        name="hgrn_scan",
    )(q, kf, bf, v, q, kb, bb, v, s0)


def _hgrn_out_kernel(of_ref, ob_ref, g_ref, x_ref, mod_ref, w_ref, o_ref):
    o = of_ref[...] + ob_ref[...]
    parts = []
    for h in range(HEADS):
        oh = o[:, h * HEAD_DIM:(h + 1) * HEAD_DIM]
        ms = jnp.mean(oh * oh, axis=-1, keepdims=True)
        parts.append(oh * lax.rsqrt(ms + EPS))
    on = jnp.concatenate(parts, axis=1) * g_ref[...].astype(F32)
    o_ref[...] = x_ref[...] + mod_ref[0][2:3] * _dot(on.astype(BF16), w_ref[...])


def _conv_in_kernel(x_ref, mod_ref, nw_ref, w_ref, u_ref):
    d = x_ref.shape[1]
    m = mod_ref[0]
    hn = _norm_mod(x_ref[...], nw_ref[...], m[0:1], m[1:2]).astype(BF16)
    p = _dot(hn, w_ref[...])
    u_ref[...] = p[:, :d] * _sigmoid(p[:, d:])


def _conv_out_kernel(up_ref, u_ref, un_ref, x_ref, mod_ref, wdw_ref, bdw_ref, lng_ref, lnb_ref, w_ref,
                     o_ref, ext_ref, sh_ref, wb_ref, cv_ref, *, n_lat_tiles, tiles_per_lat_seq):
    i = pl.program_id(0)
    tm, d = u_ref.shape
    pos = i % tiles_per_lat_seq
    is_lat = i < n_lat_tiles
    first = jnp.logical_or(jnp.logical_not(is_lat), pos == 0)
    last = jnp.logical_or(jnp.logical_not(is_lat), pos == tiles_per_lat_seq - 1)
    ext_ref[0:HALO, :] = jnp.where(first, 0.0, up_ref[...])
    ext_ref[HALO:HALO + tm, :] = u_ref[...]
    ext_ref[HALO + tm:HALO + tm + HALO, :] = jnp.where(last, 0.0, un_ref[...])

    n_sh = sh_ref.shape[1]
    for s in range(1, SUBLANES):
        sh_ref[s - 1] = ext_ref[s:s + n_sh, :]

    for t in range(CONV_K):
        wb_ref[t] = jnp.broadcast_to(wdw_ref[t:t + 1, :], (SUBLANES, d))

    groups, lanes = 4, 256
    rows = groups * SUBLANES
    off = HALO - CONV_K // 2
    n_a = -(-(CONV_K + off) // SUBLANES)

    def body(r, carry):
        base = pl.multiple_of(r * rows, rows)
        for lc in range(d // lanes):
            ls = slice(lc * lanes, (lc + 1) * lanes)
            accs = [jnp.broadcast_to(bdw_ref[:, ls], (SUBLANES, lanes))] * groups
            for s in range(SUBLANES):
                taps = [(a, a * SUBLANES + s - off) for a in range(n_a)
                        if 0 <= a * SUBLANES + s - off < CONV_K]
                span = rows + max(a for a, _ in taps) * SUBLANES
                if s == 0:
                    blk = ext_ref[pl.ds(base, span), ls]
                else:
                    blk = sh_ref[s - 1, pl.ds(base, span), ls]
                for a, t in taps:
                    w = wb_ref[t, :, ls]
                    accs = [acc + w * blk[(gi + a) * SUBLANES:(gi + a + 1) * SUBLANES]
                            for gi, acc in enumerate(accs)]
            cv_ref[pl.ds(base, rows), ls] = jnp.concatenate(accs, axis=0)
        return carry

    lax.fori_loop(0, tm // rows, body, 0)

    u = cv_ref[...]
    mu = jnp.mean(u, axis=-1, keepdims=True)
    uc = u - mu
    var = jnp.mean(uc * uc, axis=-1, keepdims=True)
    y = uc * lax.rsqrt(var + EPS) * lng_ref[...] + lnb_ref[...]
    out = _dot(_silu(y).astype(BF16), w_ref[...])
    o_ref[...] = x_ref[...] + mod_ref[0][2:3] * out


def _grid_pos_embed(n_tok, dim):
    rows = n_tok // GRID_W
    rr, cc = jnp.meshgrid(jnp.arange(rows, dtype=F32), jnp.arange(GRID_W, dtype=F32), indexing='ij')
    nf = dim // 4
    omega = 1.0 / (10000.0 ** (jnp.arange(nf, dtype=F32) / nf))
    er = rr.reshape(-1)[:, None] * omega
    ec = cc.reshape(-1)[:, None] * omega
    return jnp.concatenate([jnp.sin(er), jnp.cos(er), jnp.sin(ec), jnp.cos(ec)], axis=-1)


def _lower_bounds(p):
    p = jax.nn.softmax(p.astype(F32), axis=0)
    return jnp.maximum(jnp.cumsum(p, axis=0) - p[0], 0.0)


def kernel(x_prompt, x_sample, c, state_hgrn, c_ctx, w_mod, b_mod, norm_mix, norm_mlp, hgrn_w_in, hgrn_lb_fwd, hgrn_lb_bwd, hgrn_g_norm, hgrn_w_out, conv_w_pw1, conv_w_dw, conv_b_dw, conv_ln_g, conv_ln_b, conv_w_pw2, mlp_w1, mlp_w2, final_norm):
    ctx_b, ctx_t, d = x_prompt.shape
    lat_b, lat_t, _ = x_sample.shape
    depth = w_mod.shape[0]
    d_ff = mlp_w1.shape[2]
    n_lat, n_ctx = lat_b * lat_t, ctx_b * ctx_t
    n_tok = n_lat + n_ctx
    assert d == HEADS * HEAD_DIM and lat_b < MOD_ROWS
    assert lat_t % TM == 0 and n_ctx % TM == 0 and ctx_t % CHUNK == 0 and ctx_t == TM_CONV and d_ff % TF == 0

    def mod_row(tm):
        return lambda i, *_: (jnp.where(i * tm < n_lat, (i * tm) // lat_t, lat_b), 0, 0)

    def mod_spec(tm):
        return pl.BlockSpec((1, N_MOD, d), mod_row(tm))

    def tok_spec(tm, width=d):
        return pl.BlockSpec((tm, width), lambda i, *_: (i, 0))

    def full_spec(shape):
        return pl.BlockSpec(shape, lambda i, *_: (0,) * len(shape))

    def resident_spec(shape):
        return pl.BlockSpec(shape, lambda i, *_: (0,) * len(shape), pipeline_mode=pl.Buffered(1))

    row = lambda a: a.reshape(1, -1)
    tok_f32 = jax.ShapeDtypeStruct((n_tok, d), F32)
    tok_bf16 = jax.ShapeDtypeStruct((n_tok, d), BF16)

    cvec = jnp.zeros((MOD_ROWS, d), F32).at[:lat_b].set(c).at[lat_b].set(c_ctx)
    mods = _modulation(cvec, w_mod, b_mod).reshape(depth, MOD_ROWS, N_MOD, d)

    lbs_f = _lower_bounds(hgrn_lb_fwd)
    lbs_b = _lower_bounds(hgrn_lb_bwd)

    pos = _grid_pos_embed(lat_t, d).astype(x_sample.dtype)
    n_lat_tiles = n_lat // TM
    tiles_per_lat_seq = lat_t // TM
    first_specs = [pl.BlockSpec((TM, d), lambda i: (jnp.minimum(i, n_lat_tiles - 1), 0)),
                   pl.BlockSpec((TM, d), lambda i: (i % tiles_per_lat_seq, 0)),
                   pl.BlockSpec((TM, d), lambda i: (jnp.maximum(i - n_lat_tiles, 0), 0))]
    x = None

    new_states = []
    for i in range(depth):
        mod = mods[i]
        if i % 2 == 0:
            a = i // 2
            first = i == 0
            x_args = (x_sample.reshape(n_lat, d), pos, x_prompt.reshape(n_ctx, d)) if first else (x,)
            outs = pl.pallas_call(
                functools.partial(_hgrn_in_kernel, first=first, n_lat_tiles=n_lat_tiles),
                grid=(n_tok // TM,),
                in_specs=(first_specs if first else [tok_spec(TM)]) + [
                    mod_spec(TM), full_spec((1, d)), resident_spec((d, 5 * d)),
                    full_spec((1, d)), full_spec((1, d)), full_spec((1, d))],
                out_specs=[tok_spec(TM)] * (8 if first else 7),
                out_shape=[tok_bf16, tok_bf16, tok_bf16, tok_f32, tok_bf16, tok_f32, tok_bf16]
                + ([tok_f32] if first else []),
                compiler_params=_params("arbitrary"),
                name="hgrn_in",
            )(*x_args, mod, row(norm_mix[i]), hgrn_w_in[a].astype(BF16), row(lbs_f[a]), row(lbs_b[a]),
              row(hgrn_g_norm[a]))
            q, v, kf, bf, kb, bb, g = outs[:7]
            if first:
                x = outs[7]
            of, ob, st = _scan(q, kf, bf, kb, bb, v, state_hgrn, a, lat_b=lat_b, lat_t=lat_t,
                               ctx_b=ctx_b, ctx_t=ctx_t)
            new_states.append(st)
            x = pl.pallas_call(
                _hgrn_out_kernel,
                grid=(n_tok // TM,),
                in_specs=[tok_spec(TM), tok_spec(TM), tok_spec(TM), tok_spec(TM), mod_spec(TM),
                          full_spec((d, d))],
                out_specs=tok_spec(TM),
                out_shape=tok_f32,
                compiler_params=_params("parallel"),
                name="hgrn_out",
            )(of, ob, g, x, mod, hgrn_w_out[a].astype(BF16))
        else:
            b = i // 2
            u = pl.pallas_call(
                _conv_in_kernel,
                grid=(n_tok // TM,),
                in_specs=[tok_spec(TM), mod_spec(TM), full_spec((1, d)), full_spec((d, 2 * d))],
                out_specs=tok_spec(TM),
                out_shape=tok_f32,
                compiler_params=_params("parallel"),
                name="conv_in",
            )(x, mod, row(norm_mix[i]), conv_w_pw1[b].astype(BF16))
            tm = TM_CONV
            hb = tm // HALO
            n_halo = n_tok // HALO
            wdw = jnp.zeros((CONV_K + 1, d), F32).at[:CONV_K].set(conv_w_dw[b])
            x = pl.pallas_call(
                functools.partial(_conv_out_kernel, n_lat_tiles=n_lat // tm, tiles_per_lat_seq=lat_t // tm),
                grid=(n_tok // tm,),
                in_specs=[pl.BlockSpec((HALO, d), lambda i: (jnp.maximum(i * hb - 1, 0), 0)),
                          tok_spec(tm),
                          pl.BlockSpec((HALO, d), lambda i: (jnp.minimum((i + 1) * hb, n_halo - 1), 0)),
                          tok_spec(tm), mod_spec(tm), full_spec((CONV_K + 1, d)), full_spec((1, d)),
                          full_spec((1, d)), full_spec((1, d)), full_spec((d, d))],
                out_specs=tok_spec(tm),
                out_shape=tok_f32,
                scratch_shapes=[pltpu.VMEM((tm + 2 * HALO, d), F32),
                                pltpu.VMEM((SUBLANES - 1, tm + 2 * HALO - SUBLANES, d), F32),
                                pltpu.VMEM((CONV_K, SUBLANES, d), F32),
                                pltpu.VMEM((tm, d), F32)],
                compiler_params=_params("parallel"),
                name="conv_out",
            )(u, u, u, x, mod, wdw, row(conv_b_dw[b]), row(conv_ln_g[b]), row(conv_ln_b[b]),
              conv_w_pw2[b].astype(BF16))

        final = i == depth - 1
        if final:
            out_specs = [pl.BlockSpec((TM, d), lambda i: (jnp.minimum(i, n_lat_tiles - 1), 0)),
                         pl.BlockSpec((TM, d), lambda i: (jnp.maximum(i - n_lat_tiles, 0), 0))]
            out_shape = [jax.ShapeDtypeStruct((n_lat, d), F32), jax.ShapeDtypeStruct((n_ctx, d), F32)]
        else:
            out_specs, out_shape = tok_spec(TM), tok_f32
        x = pl.pallas_call(
            functools.partial(_mlp_kernel, final=final, n_lat_tiles=n_lat_tiles),
            grid=(n_tok // TM,),
            in_specs=[tok_spec(TM), mod_spec(TM), full_spec((1, d)), full_spec((1, d)),
                      resident_spec((d, d_ff)), resident_spec((d_ff, d))],
            out_specs=out_specs,
            out_shape=out_shape,
            compiler_params=_params("arbitrary"),
            name="mlp",
        )(x, mod, row(norm_mlp[i]), row(final_norm), mlp_w1[i].astype(BF16), mlp_w2[i].astype(BF16))

    y_lat, y_ctx = x
    y_sample = y_lat.reshape(lat_b, lat_t, d)
    y_prompt = y_ctx.reshape(ctx_b, ctx_t, d)
    new_state_hgrn = jnp.stack(new_states, axis=1).astype(x_prompt.dtype)
    return (y_prompt, y_sample, new_state_hgrn)
```

```python
import functools

import jax
import jax.numpy as jnp
from jax import lax
from jax.experimental import pallas as pl
from jax.experimental.pallas import tpu as pltpu

F32 = jnp.float32
BF16 = jnp.bfloat16

N_MOD = 6
HEADS = 8
HEAD_DIM = 128
GRID_W = 64
CONV_K = 31
EPS = 1e-6
K_MAX = 1.0 - 1e-6

CHUNK = 128
SUB = 16
N_SUB = CHUNK // SUB
FAST_EXP_MAX = 108.0
SUBLANES = 8
HALO = 16
MOD_ROWS = 8

TM = 512
TM_CONV = 256
TF = 1024
VMEM_LIMIT = 56 * 1024 * 1024


def _params(*sem):
    return pltpu.CompilerParams(dimension_semantics=sem, vmem_limit_bytes=VMEM_LIMIT)


def _dot(a, b):
    return jnp.dot(a, b, preferred_element_type=F32)


def _dot_nt(a, b):
    return lax.dot_general(a, b, (((1,), (1,)), ((), ())), preferred_element_type=F32)


def _sigmoid(x):
    return 1.0 / (1.0 + jnp.exp(-x))


def _silu(x):
    return x * _sigmoid(x)


def _norm_mod(x, nw, shift, scale):
    ms = jnp.mean(x * x, axis=-1, keepdims=True)
    y = x * lax.rsqrt(ms + EPS) * nw
    return y * (1.0 + scale) + shift


def _mod_kernel(c_ref, w_ref, b_ref, o_ref):
    c = _silu(c_ref[...]).astype(BF16)
    o_ref[0] = _dot(c, w_ref[0].astype(BF16)) + b_ref[0]


def _modulation(cvec, w_mod, b_mod):
    depth, d, n = w_mod.shape
    tn = 1536
    return pl.pallas_call(
        _mod_kernel,
        grid=(depth, n // tn),
        in_specs=[
            pl.BlockSpec((MOD_ROWS, d), lambda l, j: (0, 0)),
            pl.BlockSpec((1, d, tn), lambda l, j: (l, 0, j)),
            pl.BlockSpec((1, 1, tn), lambda l, j: (l, 0, j)),
        ],
        out_specs=pl.BlockSpec((1, MOD_ROWS, tn), lambda l, j: (l, 0, j)),
        out_shape=jax.ShapeDtypeStruct((depth, MOD_ROWS, n), F32),
        compiler_params=_params("parallel", "parallel"),
        name="modulation",
    )(cvec, w_mod, b_mod.reshape(depth, 1, n))


def _mlp_kernel(x_ref, mod_ref, nw_ref, fw_ref, w1_ref, w2_ref, *o_refs, final, n_lat_tiles):
    m = mod_ref[0]
    x = x_ref[...]
    hn = _norm_mod(x, nw_ref[...], m[3:4], m[4:5]).astype(BF16)
    acc = None
    for j in range(w1_ref.shape[1] // TF):
        h = _dot(hn, w1_ref[:, j * TF:(j + 1) * TF])
        h = jnp.square(jnp.maximum(h, 0.0)).astype(BF16)
        part = _dot(h, w2_ref[j * TF:(j + 1) * TF, :])
        acc = part if acc is None else acc + part
    y = x + m[5:6] * acc
    if not final:
        o_refs[0][...] = y
        return
    ms = jnp.mean(y * y, axis=-1, keepdims=True)
    y = y * lax.rsqrt(ms + EPS) * fw_ref[...]
    lat_ref, ctx_ref = o_refs
    is_lat = pl.program_id(0) < n_lat_tiles

    @pl.when(is_lat)
    def _():
        lat_ref[...] = y

    @pl.when(jnp.logical_not(is_lat))
    def _():
        ctx_ref[...] = y


def _hgrn_in_kernel(*refs, first, n_lat_tiles):
    if first:
        (xs_ref, prow_ref, pcol_ref, xp_ref, mod_ref, nw_ref, w_ref, lbf_ref, lbb_ref, gn_ref,
         q_ref, v_ref, kf_ref, bf_ref, kb_ref, bb_ref, g_ref, x0_ref) = refs
        is_lat = pl.program_id(0) < n_lat_tiles
        prow, pcol = prow_ref[...], pcol_ref[...]
        n_rows, half = prow.shape
        left = jnp.concatenate([jnp.broadcast_to(prow[r:r + 1], (GRID_W, half)) for r in range(n_rows)], axis=0)
        pos = jnp.concatenate([left, jnp.concatenate([pcol] * n_rows, axis=0)], axis=1)
        x = jnp.where(is_lat, xs_ref[...] + pos, xp_ref[...])
        x0_ref[...] = x
    else:
        (x_ref, mod_ref, nw_ref, w_ref, lbf_ref, lbb_ref, gn_ref,
         q_ref, v_ref, kf_ref, bf_ref, kb_ref, bb_ref, g_ref) = refs
        x = x_ref[...]
    tm, d = x.shape
    m = mod_ref[0]
    hn = _norm_mod(x, nw_ref[...], m[0:1], m[1:2]).astype(BF16)

    def proj(j):
        return _dot(hn, w_ref[:, j * d:(j + 1) * d])

    def gates(j, lb_ref, k_ref, b_ref, reverse):
        k = jnp.minimum((1.0 - lb_ref[...]) * _sigmoid(-proj(j)), K_MAX)
        k_ref[...] = k.astype(BF16)
        lg = jnp.log2(1.0 - k)
        row = lax.broadcasted_iota(jnp.int32, (CHUNK, CHUNK), 0)
        col = lax.broadcasted_iota(jnp.int32, (CHUNK, CHUNK), 1)
        tri = jnp.where((col >= row) if reverse else (col <= row), 1.0, 0.0).astype(BF16)
        for r in range(tm // CHUNK):
            x = lg[r * CHUNK:(r + 1) * CHUNK]
            hi = x.astype(BF16)
            lo = (x - hi.astype(F32)).astype(BF16)
            b_ref[r * CHUNK:(r + 1) * CHUNK, :] = _dot(tri, hi) + _dot(tri, lo)

    q_ref[...] = (_silu(proj(0)) * (HEAD_DIM ** -0.5)).astype(BF16)
    v_ref[...] = proj(1).astype(BF16)
    gates(2, lbf_ref, kf_ref, bf_ref, False)
    gates(3, lbb_ref, kb_ref, bb_ref, True)
    g_ref[...] = (gn_ref[...] * _silu(proj(4))).astype(BF16)


def _block_edges(b, reverse):
    edge = 0 if reverse else SUB - 1
    far = jnp.concatenate([b[i * SUB + edge:i * SUB + edge + 1] for i in range(N_SUB)], axis=0)
    zero = jnp.zeros_like(far[:1])
    ref = jnp.concatenate([far[1:], zero] if reverse else [zero, far[:-1]], axis=0)
    return far, ref


def _key_order(x, reverse):
    if not reverse:
        return x
    return jnp.concatenate([x[i * SUB:(i + 1) * SUB] for i in reversed(range(N_SUB))], axis=0)


def _scan_scores(q, k, b, reverse):
    q = q.astype(F32)
    k = k.astype(F32)
    row = lax.broadcasted_iota(jnp.int32, (CHUNK, CHUNK), 0)
    col = lax.broadcasted_iota(jnp.int32, (CHUNK, CHUNK), 1)
    if reverse:
        key = (N_SUB - 1 - col // SUB) * SUB + col % SUB
        keep = key >= row
    else:
        keep = col <= row
    far, ref = _block_edges(b, reverse)
    qs, ks = [], []
    for i in range(N_SUB):
        rows = slice(i * SUB, (i + 1) * SUB)
        qs.append((q[rows] * jnp.exp2(b[rows] - ref[i:i + 1])).astype(BF16))
        ks.append(k[rows] * jnp.exp2(far[i:i + 1] - b[rows]))
    blocks = []
    for i in range(N_SUB):
        scale = jnp.exp2(ref[i:i + 1] - far)
        sources = range(N_SUB - 1, i - 1, -1) if reverse else range(i + 1)
        keys = jnp.concatenate([(ks[j] * scale[j:j + 1]).astype(BF16) for j in sources], axis=0)
        a = _dot_nt(qs[i], keys)
        if len(sources) < N_SUB:
            a = jnp.concatenate([a, jnp.zeros((SUB, CHUNK - len(sources) * SUB), F32)], axis=1)
        blocks.append(a)
    return jnp.where(keep, jnp.concatenate(blocks, axis=0), 0.0)


def _max_block_exponent(b_ref, reverse):
    edge = 0 if reverse else SUB - 1
    far = [b_ref[i * SUB + edge:i * SUB + edge + 1, :] for i in range(N_SUB)]
    if reverse:
        worst = -far[N_SUB - 1]
        for i in range(N_SUB - 1):
            worst = jnp.maximum(worst, far[i + 1] - far[i])
    else:
        worst = -far[0]
        for i in range(1, N_SUB):
            worst = jnp.maximum(worst, far[i - 1] - far[i])
    return jnp.max(worst)


def _exact_diag_blocks(q_ref, k_ref, b_ref, a_ref, reverse):
    lane = lax.broadcasted_iota(jnp.int32, (SUB, HEAD_DIM), 1)
    rowi = lax.broadcasted_iota(jnp.int32, (SUB, HEAD_DIM), 0)

    def block(idx, carry):
        h = idx // N_SUB
        i = idx % N_SUB
        r0 = pl.multiple_of(i * SUB, SUB)
        c0 = (N_SUB - 1 - i) * SUB if reverse else r0
        lanes = pl.ds(pl.multiple_of(h * HEAD_DIM, HEAD_DIM), HEAD_DIM)
        qi = q_ref[pl.ds(r0, SUB), lanes].astype(F32)
        ki = k_ref[pl.ds(r0, SUB), lanes].astype(F32)
        bi = b_ref[pl.ds(r0, SUB), lanes]

        def column(s, acc):
            pick = rowi == s
            ks = jnp.sum(jnp.where(pick, ki, 0.0), axis=0, keepdims=True)
            bs = jnp.sum(jnp.where(pick, bi, 0.0), axis=0, keepdims=True)
            w = jnp.exp2(jnp.minimum(bi - bs, 0.0))
            c = jnp.sum(qi * ks * w, axis=-1, keepdims=True)
            return jnp.where(lane == c0 + s, c, acc)

        acc = lax.fori_loop(0, SUB, column, jnp.zeros((SUB, HEAD_DIM), F32))
        keep = (lane - c0 >= rowi) if reverse else (lane - c0 <= rowi)
        in_block = jnp.logical_and(lane >= c0, lane < c0 + SUB)
        old = a_ref[h, pl.ds(r0, SUB), :]
        a_ref[h, pl.ds(r0, SUB), :] = jnp.where(in_block, jnp.where(keep, acc, 0.0), old)
        return carry

    lax.fori_loop(0, HEADS * N_SUB, block, 0)


def _scan_apply(q, k, b, v, a, st_ref, reverse):
    b_last = b[0:1] if reverse else b[CHUNK - 1:CHUNK]
    st = st_ref[...]
    vt = _key_order(v, reverse).T
    lhs = jnp.concatenate([(q.astype(F32) * jnp.exp2(b)).astype(BF16), a.astype(BF16)], axis=1)
    rhs = jnp.concatenate([st.astype(BF16), vt], axis=1)
    k_end = _key_order((k.astype(F32) * jnp.exp2(b_last - b)).astype(BF16), reverse)
    st_ref[...] = st * jnp.exp2(b_last) + _dot(vt, k_end)
    return _dot_nt(lhs, rhs)


def _scan_kernel(qf_ref, kf_ref, bf_ref, vf_ref, qb_ref, kb_ref, bb_ref, vb_ref, s0_ref,
                 of_ref, ob_ref, sout_ref, st_ref, a_ref, *, lat_chunks, lat_nc, ctx_nc):
    g = pl.program_id(0)
    is_lat = g < lat_chunks
    c = jnp.where(is_lat, g % lat_nc, (g - lat_chunks) % ctx_nc)
    last = c == jnp.where(is_lat, lat_nc, ctx_nc) - 1

    @pl.when(jnp.logical_and(c == 0, is_lat))
    def _():
        for d in range(2):
            for h in range(HEADS):
                st_ref[d, h] = s0_ref[0, 0, d, h].T

    @pl.when(jnp.logical_and(c == 0, jnp.logical_not(is_lat)))
    def _():
        st_ref[...] = jnp.zeros_like(st_ref)

    dirs = ((qf_ref, kf_ref, bf_ref, vf_ref, of_ref, False), (qb_ref, kb_ref, bb_ref, vb_ref, ob_ref, True))
    head = lambda h: slice(h * HEAD_DIM, (h + 1) * HEAD_DIM)

    for h in range(HEADS):
        for d, (q_ref, k_ref, b_ref, _, _, reverse) in enumerate(dirs):
            a_ref[d, h] = _scan_scores(q_ref[:, head(h)], k_ref[:, head(h)], b_ref[:, head(h)], reverse)

    worst = jnp.maximum(_max_block_exponent(bf_ref, False), _max_block_exponent(bb_ref, True))

    @pl.when(worst > FAST_EXP_MAX)
    def _():
        for d, (q_ref, k_ref, b_ref, _, _, reverse) in enumerate(dirs):
            _exact_diag_blocks(q_ref, k_ref, b_ref, a_ref.at[d], reverse)

    for h in range(HEADS):
        for d, (q_ref, k_ref, b_ref, v_ref, o_ref, reverse) in enumerate(dirs):
            o_ref[:, head(h)] = _scan_apply(q_ref[:, head(h)], k_ref[:, head(h)], b_ref[:, head(h)],
                                            v_ref[:, head(h)], a_ref[d, h], st_ref.at[d, h], reverse)

    @pl.when(jnp.logical_and(last, jnp.logical_not(is_lat)))
    def _():
        for d in range(2):
            for h in range(HEADS):
                sout_ref[0, d, h] = st_ref[d, h].T


def _scan(q, kf, bf, kb, bb, v, s0, layer, *, lat_b, lat_t, ctx_b, ctx_t):
    n_tok, d = q.shape
    lat_nc, ctx_nc = lat_t // CHUNK, ctx_t // CHUNK
    lat_chunks = lat_b * lat_nc

    def seq_of(g):
        is_lat = g < lat_chunks
        cs = (g - lat_chunks) // ctx_nc
        start = jnp.where(is_lat, (g // lat_nc) * lat_nc, lat_chunks + cs * ctx_nc)
        return start, jnp.where(is_lat, lat_nc, ctx_nc), jnp.where(is_lat, 0, cs)

    def fwd(g):
        return (g, 0)

    def bwd(g):
        start, nc, _ = seq_of(g)
        return (2 * start + nc - 1 - g, 0)

    blk = lambda im: pl.BlockSpec((CHUNK, d), im)
    state_blk = (1, 2, HEADS, HEAD_DIM, HEAD_DIM)
    return pl.pallas_call(
        functools.partial(_scan_kernel, lat_chunks=lat_chunks, lat_nc=lat_nc, ctx_nc=ctx_nc),
        grid=(n_tok // CHUNK,),
        in_specs=[blk(fwd), blk(fwd), blk(fwd), blk(fwd), blk(bwd), blk(bwd), blk(bwd), blk(bwd),
                  pl.BlockSpec((1, 1) + state_blk[1:],
                               lambda g: (jnp.minimum(g // lat_nc, lat_b - 1), layer, 0, 0, 0, 0))],
        out_specs=[blk(fwd), blk(bwd), pl.BlockSpec(state_blk, lambda g: (seq_of(g)[2], 0, 0, 0, 0))],
        out_shape=[jax.ShapeDtypeStruct((n_tok, d), F32), jax.ShapeDtypeStruct((n_tok, d), F32),
                   jax.ShapeDtypeStruct((ctx_b,) + state_blk[1:], F32)],
        scratch_shapes=[pltpu.VMEM((2, HEADS, HEAD_DIM, HEAD_DIM), F32),
                        pltpu.VMEM((2, HEADS, CHUNK, CHUNK), F32)],
        compiler_params=_params("arbitrary"),
        name="hgrn_scan",
    )(q, kf, bf, v, q, kb, bb, v, s0)


def _hgrn_out_kernel(of_ref, ob_ref, g_ref, x_ref, mod_ref, w_ref, o_ref):
    o = of_ref[...] + ob_ref[...]
    parts = []
    for h in range(HEADS):
        oh = o[:, h * HEAD_DIM:(h + 1) * HEAD_DIM]
        ms = jnp.mean(oh * oh, axis=-1, keepdims=True)
        parts.append(oh * lax.rsqrt(ms + EPS))
    on = jnp.concatenate(parts, axis=1) * g_ref[...].astype(F32)
    o_ref[...] = x_ref[...] + mod_ref[0][2:3] * _dot(on.astype(BF16), w_ref[...])


def _conv_in_kernel(x_ref, mod_ref, nw_ref, w_ref, u_ref):
    d = x_ref.shape[1]
    m = mod_ref[0]
    hn = _norm_mod(x_ref[...], nw_ref[...], m[0:1], m[1:2]).astype(BF16)
    p = _dot(hn, w_ref[...])
    u_ref[...] = p[:, :d] * _sigmoid(p[:, d:])


def _conv_out_kernel(up_ref, u_ref, un_ref, x_ref, mod_ref, wdw_ref, bdw_ref, lng_ref, lnb_ref, w_ref,
                     o_ref, ext_ref, sh_ref, wb_ref, cv_ref, *, n_lat_tiles, tiles_per_lat_seq):
    i = pl.program_id(0)
    tm, d = u_ref.shape
    pos = i % tiles_per_lat_seq
    is_lat = i < n_lat_tiles
    first = jnp.logical_or(jnp.logical_not(is_lat), pos == 0)
    last = jnp.logical_or(jnp.logical_not(is_lat), pos == tiles_per_lat_seq - 1)
    ext_ref[0:HALO, :] = jnp.where(first, 0.0, up_ref[...])
    ext_ref[HALO:HALO + tm, :] = u_ref[...]
    ext_ref[HALO + tm:HALO + tm + HALO, :] = jnp.where(last, 0.0, un_ref[...])

    n_sh = sh_ref.shape[1]
    for s in range(1, SUBLANES):
        sh_ref[s - 1] = ext_ref[s:s + n_sh, :]

    @pl.when(i == 0)
    def _():
        for t in range(CONV_K):
            wb_ref[t] = jnp.broadcast_to(wdw_ref[t:t + 1, :], (SUBLANES, d))

    groups, lanes = 8, 128
    rows = groups * SUBLANES
    off = HALO - CONV_K // 2
    n_a = -(-(CONV_K + off) // SUBLANES)

    def body(r, carry):
        base = pl.multiple_of(r * rows, rows)
        for lc in range(d // lanes):
            ls = slice(lc * lanes, (lc + 1) * lanes)
            accs = [jnp.broadcast_to(bdw_ref[:, ls], (SUBLANES, lanes))] * groups
            for s in range(SUBLANES):
                taps = [(a, a * SUBLANES + s - off) for a in range(n_a)
                        if 0 <= a * SUBLANES + s - off < CONV_K]
                span = rows + max(a for a, _ in taps) * SUBLANES
                if s == 0:
                    blk = ext_ref[pl.ds(base, span), ls]
                else:
                    blk = sh_ref[s - 1, pl.ds(base, span), ls]
                for a, t in taps:
                    w = wb_ref[t, :, ls]
                    accs = [acc + w * blk[(gi + a) * SUBLANES:(gi + a + 1) * SUBLANES]
                            for gi, acc in enumerate(accs)]
            cv_ref[pl.ds(base, rows), ls] = jnp.concatenate(accs, axis=0)
        return carry

    lax.fori_loop(0, tm // rows, body, 0)

    u = cv_ref[...]
    mu = jnp.mean(u, axis=-1, keepdims=True)
    uc = u - mu
    var = jnp.mean(uc * uc, axis=-1, keepdims=True)
    y = uc * lax.rsqrt(var + EPS) * lng_ref[...] + lnb_ref[...]
    out = _dot(_silu(y).astype(BF16), w_ref[...])
    o_ref[...] = x_ref[...] + mod_ref[0][2:3] * out


def _grid_pos_tables(n_tok, dim):
    nf = dim // 4
    omega = 1.0 / (10000.0 ** (jnp.arange(nf, dtype=F32) / nf))
    er = jnp.arange(n_tok // GRID_W, dtype=F32)[:, None] * omega
    ec = jnp.arange(GRID_W, dtype=F32)[:, None] * omega
    return (jnp.concatenate([jnp.sin(er), jnp.cos(er)], axis=-1),
            jnp.concatenate([jnp.sin(ec), jnp.cos(ec)], axis=-1))


def _lower_bounds(p):
    p = jax.nn.softmax(p.astype(F32), axis=0)
    return jnp.maximum(jnp.cumsum(p, axis=0) - p[0], 0.0)


def kernel(x_prompt, x_sample, c, state_hgrn, c_ctx, w_mod, b_mod, norm_mix, norm_mlp, hgrn_w_in, hgrn_lb_fwd, hgrn_lb_bwd, hgrn_g_norm, hgrn_w_out, conv_w_pw1, conv_w_dw, conv_b_dw, conv_ln_g, conv_ln_b, conv_w_pw2, mlp_w1, mlp_w2, final_norm):
    ctx_b, ctx_t, d = x_prompt.shape
    lat_b, lat_t, _ = x_sample.shape
    depth = w_mod.shape[0]
    d_ff = mlp_w1.shape[2]
    n_lat, n_ctx = lat_b * lat_t, ctx_b * ctx_t
    n_tok = n_lat + n_ctx
    assert d == HEADS * HEAD_DIM and lat_b < MOD_ROWS
    assert lat_t % TM == 0 and n_ctx % TM == 0 and ctx_t % CHUNK == 0 and ctx_t == TM_CONV and d_ff % TF == 0
    assert TM % GRID_W == 0 and (TM // GRID_W) % SUBLANES == 0

    def mod_row(tm):
        return lambda i, *_: (jnp.where(i * tm < n_lat, (i * tm) // lat_t, lat_b), 0, 0)

    def mod_spec(tm):
        return pl.BlockSpec((1, N_MOD, d), mod_row(tm))

    def tok_spec(tm, width=d):
        return pl.BlockSpec((tm, width), lambda i, *_: (i, 0))

    def full_spec(shape):
        return pl.BlockSpec(shape, lambda i, *_: (0,) * len(shape))

    def resident_spec(shape):
        return pl.BlockSpec(shape, lambda i, *_: (0,) * len(shape), pipeline_mode=pl.Buffered(1))

    row = lambda a: a.reshape(1, -1)
    tok_f32 = jax.ShapeDtypeStruct((n_tok, d), F32)
    tok_bf16 = jax.ShapeDtypeStruct((n_tok, d), BF16)

    cvec = jnp.zeros((MOD_ROWS, d), F32).at[:lat_b].set(c).at[lat_b].set(c_ctx)
    mods = _modulation(cvec, w_mod, b_mod).reshape(depth, MOD_ROWS, N_MOD, d)

    lbs_f = _lower_bounds(hgrn_lb_fwd)
    lbs_b = _lower_bounds(hgrn_lb_bwd)

    pos_row, pos_col = _grid_pos_tables(lat_t, d)
    n_lat_tiles = n_lat // TM
    tiles_per_lat_seq = lat_t // TM
    first_specs = [pl.BlockSpec((TM, d), lambda i: (jnp.minimum(i, n_lat_tiles - 1), 0)),
                   pl.BlockSpec((TM // GRID_W, d // 2), lambda i: (i % tiles_per_lat_seq, 0)),
                   full_spec((GRID_W, d // 2)),
                   pl.BlockSpec((TM, d), lambda i: (jnp.maximum(i - n_lat_tiles, 0), 0))]
    x = None

    new_states = []
    for i in range(depth):
        mod = mods[i]
        if i % 2 == 0:
            a = i // 2
            first = i == 0
            x_args = ((x_sample.reshape(n_lat, d), pos_row, pos_col, x_prompt.reshape(n_ctx, d))
                      if first else (x,))
            outs = pl.pallas_call(
                functools.partial(_hgrn_in_kernel, first=first, n_lat_tiles=n_lat_tiles),
                grid=(n_tok // TM,),
                in_specs=(first_specs if first else [tok_spec(TM)]) + [
                    mod_spec(TM), full_spec((1, d)), resident_spec((d, 5 * d)),
                    full_spec((1, d)), full_spec((1, d)), full_spec((1, d))],
                out_specs=[tok_spec(TM)] * (8 if first else 7),
                out_shape=[tok_bf16, tok_bf16, tok_bf16, tok_f32, tok_bf16, tok_f32, tok_bf16]
                + ([tok_f32] if first else []),
                compiler_params=_params("arbitrary"),
                name="hgrn_in",
            )(*x_args, mod, row(norm_mix[i]), hgrn_w_in[a].astype(BF16), row(lbs_f[a]), row(lbs_b[a]),
              row(hgrn_g_norm[a]))
            q, v, kf, bf, kb, bb, g = outs[:7]
            if first:
                x = outs[7]
            of, ob, st = _scan(q, kf, bf, kb, bb, v, state_hgrn, a, lat_b=lat_b, lat_t=lat_t,
                               ctx_b=ctx_b, ctx_t=ctx_t)
            new_states.append(st)
            x = pl.pallas_call(
                _hgrn_out_kernel,
                grid=(n_tok // TM,),
                in_specs=[tok_spec(TM), tok_spec(TM), tok_spec(TM), tok_spec(TM), mod_spec(TM),
                          full_spec((d, d))],
                out_specs=tok_spec(TM),
                out_shape=tok_f32,
                compiler_params=_params("parallel"),
                name="hgrn_out",
            )(of, ob, g, x, mod, hgrn_w_out[a].astype(BF16))
        else:
            b = i // 2
            u = pl.pallas_call(
                _conv_in_kernel,
                grid=(n_tok // TM,),
                in_specs=[tok_spec(TM), mod_spec(TM), full_spec((1, d)), full_spec((d, 2 * d))],
                out_specs=tok_spec(TM),
                out_shape=tok_f32,
                compiler_params=_params("parallel"),
                name="conv_in",
            )(x, mod, row(norm_mix[i]), conv_w_pw1[b].astype(BF16))
            tm = TM_CONV
            hb = tm // HALO
            n_halo = n_tok // HALO
            wdw = jnp.zeros((CONV_K + 1, d), F32).at[:CONV_K].set(conv_w_dw[b])
            x = pl.pallas_call(
                functools.partial(_conv_out_kernel, n_lat_tiles=n_lat // tm, tiles_per_lat_seq=lat_t // tm),
                grid=(n_tok // tm,),
                in_specs=[pl.BlockSpec((HALO, d), lambda i: (jnp.maximum(i * hb - 1, 0), 0)),
                          tok_spec(tm),
                          pl.BlockSpec((HALO, d), lambda i: (jnp.minimum((i + 1) * hb, n_halo - 1), 0)),
                          tok_spec(tm), mod_spec(tm), full_spec((CONV_K + 1, d)), full_spec((1, d)),
                          full_spec((1, d)), full_spec((1, d)), full_spec((d, d))],
                out_specs=tok_spec(tm),
                out_shape=tok_f32,
                scratch_shapes=[pltpu.VMEM((tm + 2 * HALO, d), F32),
                                pltpu.VMEM((SUBLANES - 1, tm + 2 * HALO - SUBLANES, d), F32),
                                pltpu.VMEM((CONV_K, SUBLANES, d), F32),
                                pltpu.VMEM((tm, d), F32)],
                compiler_params=_params("arbitrary"),
                name="conv_out",
            )(u, u, u, x, mod, wdw, row(conv_b_dw[b]), row(conv_ln_g[b]), row(conv_ln_b[b]),
              conv_w_pw2[b].astype(BF16))

        final = i == depth - 1
        if final:
            out_specs = [pl.BlockSpec((TM, d), lambda i: (jnp.minimum(i, n_lat_tiles - 1), 0)),
                         pl.BlockSpec((TM, d), lambda i: (jnp.maximum(i - n_lat_tiles, 0), 0))]
            out_shape = [jax.ShapeDtypeStruct((n_lat, d), F32), jax.ShapeDtypeStruct((n_ctx, d), F32)]
        else:
            out_specs, out_shape = tok_spec(TM), tok_f32
        x = pl.pallas_call(
            functools.partial(_mlp_kernel, final=final, n_lat_tiles=n_lat_tiles),
            grid=(n_tok // TM,),
            in_specs=[tok_spec(TM), mod_spec(TM), full_spec((1, d)), full_spec((1, d)),
                      resident_spec((d, d_ff)), resident_spec((d_ff, d))],
            out_specs=out_specs,
            out_shape=out_shape,
            compiler_params=_params("arbitrary"),
            name="mlp",
        )(x, mod, row(norm_mlp[i]), row(final_norm), mlp_w1[i].astype(BF16), mlp_w2[i].astype(BF16))

    y_lat, y_ctx = x
    y_sample = y_lat.reshape(lat_b, lat_t, d)
    y_prompt = y_ctx.reshape(ctx_b, ctx_t, d)
    new_state_hgrn = jnp.stack(new_states, axis=1).astype(x_prompt.dtype)
    return (y_prompt, y_sample, new_state_hgrn)
```

```python
import functools

import jax
import jax.numpy as jnp
from jax import lax
from jax.experimental import pallas as pl
from jax.experimental.pallas import tpu as pltpu

F32 = jnp.float32
BF16 = jnp.bfloat16

N_MOD = 6
HEADS = 8
HEAD_DIM = 128
GRID_W = 64
CONV_K = 31
EPS = 1e-6
K_MAX = 1.0 - 1e-6

CHUNK = 128
SUB = 16
N_SUB = CHUNK // SUB
FAST_EXP_MAX = 108.0
SUBLANES = 8
HALO = 16
MOD_ROWS = 8

TM = 512
TM_CONV = 256
TF = 1024
VMEM_LIMIT = 56 * 1024 * 1024


def _params(*sem):
    return pltpu.CompilerParams(dimension_semantics=sem, vmem_limit_bytes=VMEM_LIMIT)


def _dot(a, b):
    return jnp.dot(a, b, preferred_element_type=F32)


def _dot_nt(a, b):
    return lax.dot_general(a, b, (((1,), (1,)), ((), ())), preferred_element_type=F32)


def _sigmoid(x):
    return 1.0 / (1.0 + jnp.exp(-x))


def _silu(x):
    return x * _sigmoid(x)


def _norm_mod(x, nw, shift, scale):
    ms = jnp.mean(x * x, axis=-1, keepdims=True)
    y = x * lax.rsqrt(ms + EPS) * nw
    return y * (1.0 + scale) + shift


def _mod_kernel(c_ref, w_ref, b_ref, o_ref):
    c = _silu(c_ref[...]).astype(BF16)
    o_ref[0] = _dot(c, w_ref[0].astype(BF16)) + b_ref[0]


def _modulation(cvec, w_mod, b_mod):
    depth, d, n = w_mod.shape
    tn = 1536
    return pl.pallas_call(
        _mod_kernel,
        grid=(depth, n // tn),
        in_specs=[
            pl.BlockSpec((MOD_ROWS, d), lambda l, j: (0, 0)),
            pl.BlockSpec((1, d, tn), lambda l, j: (l, 0, j)),
            pl.BlockSpec((1, 1, tn), lambda l, j: (l, 0, j)),
        ],
        out_specs=pl.BlockSpec((1, MOD_ROWS, tn), lambda l, j: (l, 0, j)),
        out_shape=jax.ShapeDtypeStruct((depth, MOD_ROWS, n), F32),
        compiler_params=_params("parallel", "parallel"),
        name="modulation",
    )(cvec, w_mod, b_mod.reshape(depth, 1, n))


def _mlp_stage(x, m, nw, w1_ref, w2_ref):
    hn = _norm_mod(x, nw, m[3:4], m[4:5]).astype(BF16)
    acc = None
    for j in range(w1_ref.shape[1] // TF):
        h = _dot(hn, w1_ref[:, j * TF:(j + 1) * TF])
        h = jnp.square(jnp.maximum(h, 0.0)).astype(BF16)
        part = _dot(h, w2_ref[j * TF:(j + 1) * TF, :])
        acc = part if acc is None else acc + part
    return x + m[5:6] * acc


def _mlp_kernel(x_ref, mod_ref, nw_ref, fw_ref, w1_ref, w2_ref, *o_refs, final, n_lat_tiles):
    y = _mlp_stage(x_ref[...], mod_ref[0], nw_ref[...], w1_ref, w2_ref)
    if not final:
        o_refs[0][...] = y
        return
    ms = jnp.mean(y * y, axis=-1, keepdims=True)
    y = y * lax.rsqrt(ms + EPS) * fw_ref[...]
    lat_ref, ctx_ref = o_refs
    is_lat = pl.program_id(0) < n_lat_tiles

    @pl.when(is_lat)
    def _():
        lat_ref[...] = y

    @pl.when(jnp.logical_not(is_lat))
    def _():
        ctx_ref[...] = y


def _hgrn_in_kernel(*refs, first, n_lat_tiles):
    if first:
        (xs_ref, prow_ref, pcol_ref, xp_ref, mod_ref, nw_ref, w_ref, lbf_ref, lbb_ref, gn_ref,
         q_ref, v_ref, kf_ref, bf_ref, kb_ref, bb_ref, g_ref, x0_ref) = refs
        is_lat = pl.program_id(0) < n_lat_tiles
        prow, pcol = prow_ref[...], pcol_ref[...]
        n_rows, half = prow.shape
        left = jnp.concatenate([jnp.broadcast_to(prow[r:r + 1], (GRID_W, half)) for r in range(n_rows)], axis=0)
        pos = jnp.concatenate([left, jnp.concatenate([pcol] * n_rows, axis=0)], axis=1)
        x = jnp.where(is_lat, xs_ref[...] + pos, xp_ref[...])
        x0_ref[...] = x
    else:
        (x_ref, mod_ref, nw_ref, w_ref, lbf_ref, lbb_ref, gn_ref,
         q_ref, v_ref, kf_ref, bf_ref, kb_ref, bb_ref, g_ref) = refs
        x = x_ref[...]
    tm, d = x.shape
    m = mod_ref[0]
    hn = _norm_mod(x, nw_ref[...], m[0:1], m[1:2]).astype(BF16)

    def proj(j):
        return _dot(hn, w_ref[:, j * d:(j + 1) * d])

    def gates(j, lb_ref, k_ref, b_ref, reverse):
        k = jnp.minimum((1.0 - lb_ref[...]) * _sigmoid(-proj(j)), K_MAX)
        k_ref[...] = k.astype(BF16)
        lg = jnp.log2(1.0 - k)
        row = lax.broadcasted_iota(jnp.int32, (CHUNK, CHUNK), 0)
        col = lax.broadcasted_iota(jnp.int32, (CHUNK, CHUNK), 1)
        tri = jnp.where((col >= row) if reverse else (col <= row), 1.0, 0.0).astype(BF16)
        tri2 = jnp.concatenate([tri, tri], axis=1)
        for r in range(tm // CHUNK):
            x = lg[r * CHUNK:(r + 1) * CHUNK]
            hi = x.astype(BF16)
            lo = (x - hi.astype(F32)).astype(BF16)
            b_ref[r * CHUNK:(r + 1) * CHUNK, :] = _dot(tri2, jnp.concatenate([hi, lo], axis=0))

    gates(2, lbf_ref, kf_ref, bf_ref, False)
    gates(3, lbb_ref, kb_ref, bb_ref, True)
    q_ref[...] = (_silu(proj(0)) * (HEAD_DIM ** -0.5)).astype(BF16)
    g_ref[...] = (gn_ref[...] * _silu(proj(4))).astype(BF16)
    v_ref[...] = proj(1).astype(BF16)


def _block_edges(b, reverse):
    edge = 0 if reverse else SUB - 1
    far = jnp.concatenate([b[i * SUB + edge:i * SUB + edge + 1] for i in range(N_SUB)], axis=0)
    zero = jnp.zeros_like(far[:1])
    ref = jnp.concatenate([far[1:], zero] if reverse else [zero, far[:-1]], axis=0)
    return far, ref


def _key_order(x, reverse):
    if not reverse:
        return x
    return jnp.concatenate([x[i * SUB:(i + 1) * SUB] for i in reversed(range(N_SUB))], axis=0)


def _scan_scores(q, k, b, reverse):
    q = q.astype(F32)
    k = k.astype(F32)
    row = lax.broadcasted_iota(jnp.int32, (CHUNK, CHUNK), 0)
    col = lax.broadcasted_iota(jnp.int32, (CHUNK, CHUNK), 1)
    if reverse:
        key = (N_SUB - 1 - col // SUB) * SUB + col % SUB
        keep = key >= row
    else:
        keep = col <= row
    far, ref = _block_edges(b, reverse)
    qs, ks = [], []
    for i in range(N_SUB):
        rows = slice(i * SUB, (i + 1) * SUB)
        qs.append((q[rows] * jnp.exp2(b[rows] - ref[i:i + 1])).astype(BF16))
        ks.append(k[rows] * jnp.exp2(far[i:i + 1] - b[rows]))
    blocks = []
    for i in range(N_SUB):
        scale = jnp.exp2(ref[i:i + 1] - far)
        sources = range(N_SUB - 1, i - 1, -1) if reverse else range(i + 1)
        keys = jnp.concatenate([(ks[j] * scale[j:j + 1]).astype(BF16) for j in sources], axis=0)
        a = _dot_nt(qs[i], keys)
        if len(sources) < N_SUB:
            a = jnp.concatenate([a, jnp.zeros((SUB, CHUNK - len(sources) * SUB), F32)], axis=1)
        blocks.append(a)
    return jnp.where(keep, jnp.concatenate(blocks, axis=0), 0.0)


def _max_block_exponent(b_ref, reverse):
    edge = 0 if reverse else SUB - 1
    far = [b_ref[i * SUB + edge:i * SUB + edge + 1, :] for i in range(N_SUB)]
    if reverse:
        worst = -far[N_SUB - 1]
        for i in range(N_SUB - 1):
            worst = jnp.maximum(worst, far[i + 1] - far[i])
    else:
        worst = -far[0]
        for i in range(1, N_SUB):
            worst = jnp.maximum(worst, far[i - 1] - far[i])
    return jnp.max(worst)


def _exact_diag_blocks(q_ref, k_ref, b_ref, a_ref, reverse):
    lane = lax.broadcasted_iota(jnp.int32, (SUB, HEAD_DIM), 1)
    rowi = lax.broadcasted_iota(jnp.int32, (SUB, HEAD_DIM), 0)

    def block(idx, carry):
        h = idx // N_SUB
        i = idx % N_SUB
        r0 = pl.multiple_of(i * SUB, SUB)
        c0 = (N_SUB - 1 - i) * SUB if reverse else r0
        lanes = pl.ds(pl.multiple_of(h * HEAD_DIM, HEAD_DIM), HEAD_DIM)
        qi = q_ref[pl.ds(r0, SUB), lanes].astype(F32)
        ki = k_ref[pl.ds(r0, SUB), lanes].astype(F32)
        bi = b_ref[pl.ds(r0, SUB), lanes]

        def column(s, acc):
            pick = rowi == s
            ks = jnp.sum(jnp.where(pick, ki, 0.0), axis=0, keepdims=True)
            bs = jnp.sum(jnp.where(pick, bi, 0.0), axis=0, keepdims=True)
            w = jnp.exp2(jnp.minimum(bi - bs, 0.0))
            c = jnp.sum(qi * ks * w, axis=-1, keepdims=True)
            return jnp.where(lane == c0 + s, c, acc)

        acc = lax.fori_loop(0, SUB, column, jnp.zeros((SUB, HEAD_DIM), F32))
        keep = (lane - c0 >= rowi) if reverse else (lane - c0 <= rowi)
        in_block = jnp.logical_and(lane >= c0, lane < c0 + SUB)
        old = a_ref[h, pl.ds(r0, SUB), :]
        a_ref[h, pl.ds(r0, SUB), :] = jnp.where(in_block, jnp.where(keep, acc, 0.0), old)
        return carry

    lax.fori_loop(0, HEADS * N_SUB, block, 0)


def _scan_apply(q, k, b, v, a, st_ref, reverse):
    b_last = b[0:1] if reverse else b[CHUNK - 1:CHUNK]
    st = st_ref[...]
    vt = _key_order(v, reverse).T
    lhs = jnp.concatenate([(q.astype(F32) * jnp.exp2(b)).astype(BF16), a.astype(BF16)], axis=1)
    rhs = jnp.concatenate([st.astype(BF16), vt], axis=1)
    k_end = _key_order((k.astype(F32) * jnp.exp2(b_last - b)).astype(BF16), reverse)
    st_ref[...] = st * jnp.exp2(b_last) + _dot(vt, k_end)
    return _dot_nt(lhs, rhs)


def _scan_kernel(qf_ref, kf_ref, bf_ref, vf_ref, qb_ref, kb_ref, bb_ref, vb_ref, s0_ref,
                 of_ref, ob_ref, sout_ref, st_ref, a_ref, *, lat_chunks, lat_nc, ctx_nc):
    g = pl.program_id(0)
    is_lat = g < lat_chunks
    c = jnp.where(is_lat, g % lat_nc, (g - lat_chunks) % ctx_nc)
    last = c == jnp.where(is_lat, lat_nc, ctx_nc) - 1

    @pl.when(jnp.logical_and(c == 0, is_lat))
    def _():
        for d in range(2):
            for h in range(HEADS):
                st_ref[d, h] = s0_ref[0, 0, d, h].T

    @pl.when(jnp.logical_and(c == 0, jnp.logical_not(is_lat)))
    def _():
        st_ref[...] = jnp.zeros_like(st_ref)

    dirs = ((qf_ref, kf_ref, bf_ref, vf_ref, of_ref, False), (qb_ref, kb_ref, bb_ref, vb_ref, ob_ref, True))
    head = lambda h: slice(h * HEAD_DIM, (h + 1) * HEAD_DIM)

    for h in range(HEADS):
        for d, (q_ref, k_ref, b_ref, _, _, reverse) in enumerate(dirs):
            a_ref[d, h] = _scan_scores(q_ref[:, head(h)], k_ref[:, head(h)], b_ref[:, head(h)], reverse)

    worst = jnp.maximum(_max_block_exponent(bf_ref, False), _max_block_exponent(bb_ref, True))

    @pl.when(worst > FAST_EXP_MAX)
    def _():
        for d, (q_ref, k_ref, b_ref, _, _, reverse) in enumerate(dirs):
            _exact_diag_blocks(q_ref, k_ref, b_ref, a_ref.at[d], reverse)

    for h in range(HEADS):
        for d, (q_ref, k_ref, b_ref, v_ref, o_ref, reverse) in enumerate(dirs):
            o_ref[:, head(h)] = _scan_apply(q_ref[:, head(h)], k_ref[:, head(h)], b_ref[:, head(h)],
                                            v_ref[:, head(h)], a_ref[d, h], st_ref.at[d, h], reverse)

    @pl.when(jnp.logical_and(last, jnp.logical_not(is_lat)))
    def _():
        for d in range(2):
            for h in range(HEADS):
                sout_ref[0, d, h] = st_ref[d, h].T


def _scan(q, kf, bf, kb, bb, v, s0, layer, *, lat_b, lat_t, ctx_b, ctx_t):
    n_tok, d = q.shape
    lat_nc, ctx_nc = lat_t // CHUNK, ctx_t // CHUNK
    lat_chunks = lat_b * lat_nc

    def seq_of(g):
        is_lat = g < lat_chunks
        cs = (g - lat_chunks) // ctx_nc
        start = jnp.where(is_lat, (g // lat_nc) * lat_nc, lat_chunks + cs * ctx_nc)
        return start, jnp.where(is_lat, lat_nc, ctx_nc), jnp.where(is_lat, 0, cs)

    def fwd(g):
        return (g, 0)

    def bwd(g):
        start, nc, _ = seq_of(g)
        return (2 * start + nc - 1 - g, 0)

    blk = lambda im: pl.BlockSpec((CHUNK, d), im)
    state_blk = (1, 2, HEADS, HEAD_DIM, HEAD_DIM)
    return pl.pallas_call(
        functools.partial(_scan_kernel, lat_chunks=lat_chunks, lat_nc=lat_nc, ctx_nc=ctx_nc),
        grid=(n_tok // CHUNK,),
        in_specs=[blk(fwd), blk(fwd), blk(fwd), blk(fwd), blk(bwd), blk(bwd), blk(bwd), blk(bwd),
                  pl.BlockSpec((1, 1) + state_blk[1:],
                               lambda g: (jnp.minimum(g // lat_nc, lat_b - 1), layer, 0, 0, 0, 0))],
        out_specs=[blk(fwd), blk(bwd), pl.BlockSpec(state_blk, lambda g: (seq_of(g)[2], 0, 0, 0, 0))],
        out_shape=[jax.ShapeDtypeStruct((n_tok, d), F32), jax.ShapeDtypeStruct((n_tok, d), F32),
                   jax.ShapeDtypeStruct((ctx_b,) + state_blk[1:], F32)],
        scratch_shapes=[pltpu.VMEM((2, HEADS, HEAD_DIM, HEAD_DIM), F32),
                        pltpu.VMEM((2, HEADS, CHUNK, CHUNK), F32)],
        compiler_params=_params("arbitrary"),
        name="hgrn_scan",
    )(q, kf, bf, v, q, kb, bb, v, s0)


def _hgrn_out_mlp_kernel(of_ref, ob_ref, g_ref, x_ref, mod_ref, w_ref, nw_ref, w1_ref, w2_ref, o_ref):
    m = mod_ref[0]
    o = of_ref[...] + ob_ref[...]
    parts = []
    for h in range(HEADS):
        oh = o[:, h * HEAD_DIM:(h + 1) * HEAD_DIM]
        ms = jnp.mean(oh * oh, axis=-1, keepdims=True)
        parts.append(oh * lax.rsqrt(ms + EPS))
    on = jnp.concatenate(parts, axis=1) * g_ref[...].astype(F32)
    x = x_ref[...] + m[2:3] * _dot(on.astype(BF16), w_ref[...])
    o_ref[...] = _mlp_stage(x, m, nw_ref[...], w1_ref, w2_ref)


def _conv_in_kernel(x_ref, mod_ref, nw_ref, w_ref, u_ref):
    d = x_ref.shape[1]
    m = mod_ref[0]
    hn = _norm_mod(x_ref[...], nw_ref[...], m[0:1], m[1:2]).astype(BF16)
    p = _dot(hn, w_ref[...])
    u_ref[...] = p[:, :d] * _sigmoid(p[:, d:])


def _conv_out_kernel(up_ref, u_ref, un_ref, x_ref, mod_ref, wdw_ref, bdw_ref, lng_ref, lnb_ref, w_ref,
                     o_ref, ext_ref, sh_ref, wb_ref, cv_ref, *, n_lat_tiles, tiles_per_lat_seq):
    i = pl.program_id(0)
    tm, d = u_ref.shape
    pos = i % tiles_per_lat_seq
    is_lat = i < n_lat_tiles
    first = jnp.logical_or(jnp.logical_not(is_lat), pos == 0)
    last = jnp.logical_or(jnp.logical_not(is_lat), pos == tiles_per_lat_seq - 1)
    ext_ref[0:HALO, :] = jnp.where(first, 0.0, up_ref[...])
    ext_ref[HALO:HALO + tm, :] = u_ref[...]
    ext_ref[HALO + tm:HALO + tm + HALO, :] = jnp.where(last, 0.0, un_ref[...])

    n_sh = sh_ref.shape[1]
    for s in range(1, SUBLANES):
        sh_ref[s - 1] = ext_ref[s:s + n_sh, :]

    @pl.when(i == 0)
    def _():
        for t in range(CONV_K):
            wb_ref[t] = jnp.broadcast_to(wdw_ref[t:t + 1, :], (SUBLANES, d))

    groups, lanes = 8, 128
    rows = groups * SUBLANES
    off = HALO - CONV_K // 2
    n_a = -(-(CONV_K + off) // SUBLANES)

    def body(r, carry):
        base = pl.multiple_of(r * rows, rows)
        for lc in range(d // lanes):
            ls = slice(lc * lanes, (lc + 1) * lanes)
            accs = [jnp.broadcast_to(bdw_ref[:, ls], (SUBLANES, lanes))] * groups
            for s in range(SUBLANES):
                taps = [(a, a * SUBLANES + s - off) for a in range(n_a)
                        if 0 <= a * SUBLANES + s - off < CONV_K]
                span = rows + max(a for a, _ in taps) * SUBLANES
                if s == 0:
                    blk = ext_ref[pl.ds(base, span), ls]
                else:
                    blk = sh_ref[s - 1, pl.ds(base, span), ls]
                for a, t in taps:
                    w = wb_ref[t, :, ls]
                    accs = [acc + w * blk[(gi + a) * SUBLANES:(gi + a + 1) * SUBLANES]
                            for gi, acc in enumerate(accs)]
            cv_ref[pl.ds(base, rows), ls] = jnp.concatenate(accs, axis=0)
        return carry

    lax.fori_loop(0, tm // rows, body, 0)

    u = cv_ref[...]
    mu = jnp.mean(u, axis=-1, keepdims=True)
    uc = u - mu
    var = jnp.mean(uc * uc, axis=-1, keepdims=True)
    y = uc * lax.rsqrt(var + EPS) * lng_ref[...] + lnb_ref[...]
    out = _dot(_silu(y).astype(BF16), w_ref[...])
    o_ref[...] = x_ref[...] + mod_ref[0][2:3] * out


def _grid_pos_tables(n_tok, dim):
    nf = dim // 4
    omega = 1.0 / (10000.0 ** (jnp.arange(nf, dtype=F32) / nf))
    er = jnp.arange(n_tok // GRID_W, dtype=F32)[:, None] * omega
    ec = jnp.arange(GRID_W, dtype=F32)[:, None] * omega
    return (jnp.concatenate([jnp.sin(er), jnp.cos(er)], axis=-1),
            jnp.concatenate([jnp.sin(ec), jnp.cos(ec)], axis=-1))


def _lower_bounds(p):
    p = jax.nn.softmax(p.astype(F32), axis=0)
    return jnp.maximum(jnp.cumsum(p, axis=0) - p[0], 0.0)


def kernel(x_prompt, x_sample, c, state_hgrn, c_ctx, w_mod, b_mod, norm_mix, norm_mlp, hgrn_w_in, hgrn_lb_fwd, hgrn_lb_bwd, hgrn_g_norm, hgrn_w_out, conv_w_pw1, conv_w_dw, conv_b_dw, conv_ln_g, conv_ln_b, conv_w_pw2, mlp_w1, mlp_w2, final_norm):
    ctx_b, ctx_t, d = x_prompt.shape
    lat_b, lat_t, _ = x_sample.shape
    depth = w_mod.shape[0]
    d_ff = mlp_w1.shape[2]
    n_lat, n_ctx = lat_b * lat_t, ctx_b * ctx_t
    n_tok = n_lat + n_ctx
    assert d == HEADS * HEAD_DIM and lat_b < MOD_ROWS
    assert depth % 2 == 0
    assert lat_t % TM == 0 and n_ctx % TM == 0 and ctx_t % CHUNK == 0 and ctx_t == TM_CONV and d_ff % TF == 0
    assert TM % GRID_W == 0 and (TM // GRID_W) % SUBLANES == 0

    def mod_row(tm):
        return lambda i, *_: (jnp.where(i * tm < n_lat, (i * tm) // lat_t, lat_b), 0, 0)

    def mod_spec(tm):
        return pl.BlockSpec((1, N_MOD, d), mod_row(tm))

    def tok_spec(tm, width=d):
        return pl.BlockSpec((tm, width), lambda i, *_: (i, 0))

    def full_spec(shape):
        return pl.BlockSpec(shape, lambda i, *_: (0,) * len(shape))

    def resident_spec(shape):
        return pl.BlockSpec(shape, lambda i, *_: (0,) * len(shape), pipeline_mode=pl.Buffered(1))

    row = lambda a: a.reshape(1, -1)
    tok_f32 = jax.ShapeDtypeStruct((n_tok, d), F32)
    tok_bf16 = jax.ShapeDtypeStruct((n_tok, d), BF16)

    cvec = jnp.zeros((MOD_ROWS, d), F32).at[:lat_b].set(c).at[lat_b].set(c_ctx)
    mods = _modulation(cvec, w_mod, b_mod).reshape(depth, MOD_ROWS, N_MOD, d)

    lbs_f = _lower_bounds(hgrn_lb_fwd)
    lbs_b = _lower_bounds(hgrn_lb_bwd)

    pos_row, pos_col = _grid_pos_tables(lat_t, d)
    n_lat_tiles = n_lat // TM
    tiles_per_lat_seq = lat_t // TM
    first_specs = [pl.BlockSpec((TM, d), lambda i: (jnp.minimum(i, n_lat_tiles - 1), 0)),
                   pl.BlockSpec((TM // GRID_W, d // 2), lambda i: (i % tiles_per_lat_seq, 0)),
                   full_spec((GRID_W, d // 2)),
                   pl.BlockSpec((TM, d), lambda i: (jnp.maximum(i - n_lat_tiles, 0), 0))]
    x = None

    new_states = []
    for i in range(depth):
        mod = mods[i]
        if i % 2 == 0:
            a = i // 2
            first = i == 0
            x_args = ((x_sample.reshape(n_lat, d), pos_row, pos_col, x_prompt.reshape(n_ctx, d))
                      if first else (x,))
            outs = pl.pallas_call(
                functools.partial(_hgrn_in_kernel, first=first, n_lat_tiles=n_lat_tiles),
                grid=(n_tok // TM,),
                in_specs=(first_specs if first else [tok_spec(TM)]) + [
                    mod_spec(TM), full_spec((1, d)), resident_spec((d, 5 * d)),
                    full_spec((1, d)), full_spec((1, d)), full_spec((1, d))],
                out_specs=[tok_spec(TM)] * (8 if first else 7),
                out_shape=[tok_bf16, tok_bf16, tok_bf16, tok_f32, tok_bf16, tok_f32, tok_bf16]
                + ([tok_f32] if first else []),
                compiler_params=_params("arbitrary"),
                name="hgrn_in",
            )(*x_args, mod, row(norm_mix[i]), hgrn_w_in[a].astype(BF16), row(lbs_f[a]), row(lbs_b[a]),
              row(hgrn_g_norm[a]))
            q, v, kf, bf, kb, bb, g = outs[:7]
            if first:
                x = outs[7]
            of, ob, st = _scan(q, kf, bf, kb, bb, v, state_hgrn, a, lat_b=lat_b, lat_t=lat_t,
                               ctx_b=ctx_b, ctx_t=ctx_t)
            new_states.append(st)
            x = pl.pallas_call(
                _hgrn_out_mlp_kernel,
                grid=(n_tok // TM,),
                in_specs=[tok_spec(TM), tok_spec(TM), tok_spec(TM), tok_spec(TM), mod_spec(TM),
                          full_spec((d, d)), full_spec((1, d)), resident_spec((d, d_ff)),
                          resident_spec((d_ff, d))],
                out_specs=tok_spec(TM),
                out_shape=tok_f32,
                compiler_params=_params("parallel"),
                name="hgrn_out_mlp",
            )(of, ob, g, x, mod, hgrn_w_out[a].astype(BF16), row(norm_mlp[i]), mlp_w1[i].astype(BF16),
              mlp_w2[i].astype(BF16))
            continue

        b = i // 2
        u = pl.pallas_call(
            _conv_in_kernel,
            grid=(n_tok // TM,),
            in_specs=[tok_spec(TM), mod_spec(TM), full_spec((1, d)), full_spec((d, 2 * d))],
            out_specs=tok_spec(TM),
            out_shape=tok_f32,
            compiler_params=_params("parallel"),
            name="conv_in",
        )(x, mod, row(norm_mix[i]), conv_w_pw1[b].astype(BF16))
        tm = TM_CONV
        hb = tm // HALO
        n_halo = n_tok // HALO
        wdw = jnp.zeros((CONV_K + 1, d), F32).at[:CONV_K].set(conv_w_dw[b])
        x = pl.pallas_call(
            functools.partial(_conv_out_kernel, n_lat_tiles=n_lat // tm, tiles_per_lat_seq=lat_t // tm),
            grid=(n_tok // tm,),
            in_specs=[pl.BlockSpec((HALO, d), lambda i: (jnp.maximum(i * hb - 1, 0), 0)),
                      tok_spec(tm),
                      pl.BlockSpec((HALO, d), lambda i: (jnp.minimum((i + 1) * hb, n_halo - 1), 0)),
                      tok_spec(tm), mod_spec(tm), full_spec((CONV_K + 1, d)), full_spec((1, d)),
                      full_spec((1, d)), full_spec((1, d)), full_spec((d, d))],
            out_specs=tok_spec(tm),
            out_shape=tok_f32,
            scratch_shapes=[pltpu.VMEM((tm + 2 * HALO, d), F32),
                            pltpu.VMEM((SUBLANES - 1, tm + 2 * HALO - SUBLANES, d), F32),
                            pltpu.VMEM((CONV_K, SUBLANES, d), F32),
                            pltpu.VMEM((tm, d), F32)],
            compiler_params=_params("arbitrary"),
            name="conv_out",
        )(u, u, u, x, mod, wdw, row(conv_b_dw[b]), row(conv_ln_g[b]), row(conv_ln_b[b]),
          conv_w_pw2[b].astype(BF16))

        final = i == depth - 1
        if final:
            out_specs = [pl.BlockSpec((TM, d), lambda i: (jnp.minimum(i, n_lat_tiles - 1), 0)),
                         pl.BlockSpec((TM, d), lambda i: (jnp.maximum(i - n_lat_tiles, 0), 0))]
            out_shape = [jax.ShapeDtypeStruct((n_lat, d), F32), jax.ShapeDtypeStruct((n_ctx, d), F32)]
        else:
            out_specs, out_shape = tok_spec(TM), tok_f32
        x = pl.pallas_call(
            functools.partial(_mlp_kernel, final=final, n_lat_tiles=n_lat_tiles),
            grid=(n_tok // TM,),
            in_specs=[tok_spec(TM), mod_spec(TM), full_spec((1, d)), full_spec((1, d)),
                      resident_spec((d, d_ff)), resident_spec((d_ff, d))],
            out_specs=out_specs,
            out_shape=out_shape,
            compiler_params=_params("arbitrary"),
            name="mlp",
        )(x, mod, row(norm_mlp[i]), row(final_norm), mlp_w1[i].astype(BF16), mlp_w2[i].astype(BF16))

    y_lat, y_ctx = x
    y_sample = y_lat.reshape(lat_b, lat_t, d)
    y_prompt = y_ctx.reshape(ctx_b, ctx_t, d)
    new_state_hgrn = jnp.stack(new_states, axis=1).astype(x_prompt.dtype)
    return (y_prompt, y_sample, new_state_hgrn)
```

```python
import functools

import jax
import jax.numpy as jnp
from jax import lax
from jax.experimental import pallas as pl
from jax.experimental.pallas import tpu as pltpu

F32 = jnp.float32
BF16 = jnp.bfloat16

N_MOD = 6
HEADS = 8
HEAD_DIM = 128
GRID_W = 64
CONV_K = 31
EPS = 1e-6
K_MAX = 1.0 - 1e-6

CHUNK = 128
SUB = 16
SUB_WIDE = 32
N_SUB = CHUNK // SUB
FAST_EXP_MAX = 108.0
SUBLANES = 8
HALO = 16
MOD_ROWS = 8

TM = 512
TM_CONV = 256
TF = 1024
VMEM_LIMIT = 56 * 1024 * 1024


def _params(*sem):
    return pltpu.CompilerParams(dimension_semantics=sem, vmem_limit_bytes=VMEM_LIMIT)


def _dot(a, b):
    return jnp.dot(a, b, preferred_element_type=F32)


def _dot_nt(a, b):
    return lax.dot_general(a, b, (((1,), (1,)), ((), ())), preferred_element_type=F32)


def _sigmoid(x):
    return 1.0 / (1.0 + jnp.exp(-x))


def _silu(x):
    return x * _sigmoid(x)


def _norm_mod(x, nw, shift, scale):
    ms = jnp.mean(x * x, axis=-1, keepdims=True)
    y = x * lax.rsqrt(ms + EPS) * nw
    return y * (1.0 + scale) + shift


def _mod_kernel(c_ref, w_ref, b_ref, o_ref):
    c = _silu(c_ref[...]).astype(BF16)
    o_ref[0] = _dot(c, w_ref[0].astype(BF16)) + b_ref[0]


def _modulation(cvec, w_mod, b_mod):
    depth, d, n = w_mod.shape
    tn = 1536
    return pl.pallas_call(
        _mod_kernel,
        grid=(depth, n // tn),
        in_specs=[
            pl.BlockSpec((MOD_ROWS, d), lambda l, j: (0, 0)),
            pl.BlockSpec((1, d, tn), lambda l, j: (l, 0, j)),
            pl.BlockSpec((1, 1, tn), lambda l, j: (l, 0, j)),
        ],
        out_specs=pl.BlockSpec((1, MOD_ROWS, tn), lambda l, j: (l, 0, j)),
        out_shape=jax.ShapeDtypeStruct((depth, MOD_ROWS, n), F32),
        compiler_params=_params("parallel", "parallel"),
        name="modulation",
    )(cvec, w_mod, b_mod.reshape(depth, 1, n))


def _mlp_stage(x, m, nw, w1_ref, w2_ref):
    hn = _norm_mod(x, nw, m[3:4], m[4:5]).astype(BF16)
    acc = None
    for j in range(w1_ref.shape[1] // TF):
        h = _dot(hn, w1_ref[:, j * TF:(j + 1) * TF])
        h = jnp.square(jnp.maximum(h, 0.0)).astype(BF16)
        part = _dot(h, w2_ref[j * TF:(j + 1) * TF, :])
        acc = part if acc is None else acc + part
    return x + m[5:6] * acc


def _mlp_kernel(x_ref, mod_ref, nw_ref, fw_ref, w1_ref, w2_ref, *o_refs, final, n_lat_tiles):
    y = _mlp_stage(x_ref[...], mod_ref[0], nw_ref[...], w1_ref, w2_ref)
    if not final:
        o_refs[0][...] = y
        return
    ms = jnp.mean(y * y, axis=-1, keepdims=True)
    y = y * lax.rsqrt(ms + EPS) * fw_ref[...]
    lat_ref, ctx_ref = o_refs
    is_lat = pl.program_id(0) < n_lat_tiles

    @pl.when(is_lat)
    def _():
        lat_ref[...] = y

    @pl.when(jnp.logical_not(is_lat))
    def _():
        ctx_ref[...] = y


def _hgrn_in_kernel(*refs, first, n_lat_tiles):
    if first:
        (xs_ref, prow_ref, pcol_ref, xp_ref, mod_ref, nw_ref, w_ref, lbf_ref, lbb_ref, gn_ref,
         q_ref, v_ref, kf_ref, bf_ref, kb_ref, bb_ref, g_ref, x0_ref) = refs
        is_lat = pl.program_id(0) < n_lat_tiles
        prow, pcol = prow_ref[...], pcol_ref[...]
        n_rows, half = prow.shape
        left = jnp.concatenate([jnp.broadcast_to(prow[r:r + 1], (GRID_W, half)) for r in range(n_rows)], axis=0)
        pos = jnp.concatenate([left, jnp.concatenate([pcol] * n_rows, axis=0)], axis=1)
        x = jnp.where(is_lat, xs_ref[...] + pos, xp_ref[...])
        x0_ref[...] = x
    else:
        (x_ref, mod_ref, nw_ref, w_ref, lbf_ref, lbb_ref, gn_ref,
         q_ref, v_ref, kf_ref, bf_ref, kb_ref, bb_ref, g_ref) = refs
        x = x_ref[...]
    tm, d = x.shape
    m = mod_ref[0]
    hn = _norm_mod(x, nw_ref[...], m[0:1], m[1:2]).astype(BF16)

    def proj(j):
        return _dot(hn, w_ref[:, j * d:(j + 1) * d])

    def gates(j, lb_ref, k_ref, b_ref, reverse):
        k = jnp.minimum((1.0 - lb_ref[...]) * _sigmoid(-proj(j)), K_MAX)
        k_ref[...] = k.astype(BF16)
        lg = jnp.log2(1.0 - k)
        row = lax.broadcasted_iota(jnp.int32, (CHUNK, CHUNK), 0)
        col = lax.broadcasted_iota(jnp.int32, (CHUNK, CHUNK), 1)
        tri = jnp.where((col >= row) if reverse else (col <= row), 1.0, 0.0).astype(BF16)
        tri2 = jnp.concatenate([tri, tri], axis=1)
        for r in range(tm // CHUNK):
            x = lg[r * CHUNK:(r + 1) * CHUNK]
            hi = x.astype(BF16)
            lo = (x - hi.astype(F32)).astype(BF16)
            b_ref[r * CHUNK:(r + 1) * CHUNK, :] = _dot(tri2, jnp.concatenate([hi, lo], axis=0))

    gates(2, lbf_ref, kf_ref, bf_ref, False)
    gates(3, lbb_ref, kb_ref, bb_ref, True)
    q_ref[...] = (_silu(proj(0)) * (HEAD_DIM ** -0.5)).astype(BF16)
    g_ref[...] = (gn_ref[...] * _silu(proj(4))).astype(BF16)
    v_ref[...] = proj(1).astype(BF16)


def _block_edges(b, reverse, sub):
    n = CHUNK // sub
    edge = 0 if reverse else sub - 1
    far = jnp.concatenate([b[i * sub + edge:i * sub + edge + 1] for i in range(n)], axis=0)
    zero = jnp.zeros_like(far[:1])
    ref = jnp.concatenate([far[1:], zero] if reverse else [zero, far[:-1]], axis=0)
    return far, ref


def _key_order(x, reverse, sub):
    if not reverse:
        return x
    return jnp.concatenate([x[i * sub:(i + 1) * sub] for i in reversed(range(CHUNK // sub))], axis=0)


def _scan_scores(q, k, b, reverse, sub):
    n = CHUNK // sub
    q = q.astype(F32)
    k = k.astype(F32)
    row = lax.broadcasted_iota(jnp.int32, (CHUNK, CHUNK), 0)
    col = lax.broadcasted_iota(jnp.int32, (CHUNK, CHUNK), 1)
    if reverse:
        key = (n - 1 - col // sub) * sub + col % sub
        keep = key >= row
    else:
        keep = col <= row
    far, ref = _block_edges(b, reverse, sub)
    qs, ks = [], []
    for i in range(n):
        rows = slice(i * sub, (i + 1) * sub)
        qs.append((q[rows] * jnp.exp2(b[rows] - ref[i:i + 1])).astype(BF16))
        ks.append(k[rows] * jnp.exp2(far[i:i + 1] - b[rows]))
    blocks = []
    for i in range(n):
        scale = jnp.exp2(ref[i:i + 1] - far)
        sources = range(n - 1, i - 1, -1) if reverse else range(i + 1)
        keys = jnp.concatenate([(ks[j] * scale[j:j + 1]).astype(BF16) for j in sources], axis=0)
        a = _dot_nt(qs[i], keys)
        if len(sources) < n:
            a = jnp.concatenate([a, jnp.zeros((sub, CHUNK - len(sources) * sub), F32)], axis=1)
        blocks.append(a)
    return jnp.where(keep, jnp.concatenate(blocks, axis=0), 0.0)


def _max_block_exponent(b_ref, reverse, sub):
    n = CHUNK // sub
    edge = 0 if reverse else sub - 1
    far = [b_ref[i * sub + edge:i * sub + edge + 1, :] for i in range(n)]
    if reverse:
        worst = -far[n - 1]
        for i in range(n - 1):
            worst = jnp.maximum(worst, far[i + 1] - far[i])
    else:
        worst = -far[0]
        for i in range(1, n):
            worst = jnp.maximum(worst, far[i - 1] - far[i])
    return jnp.max(worst)


def _exact_diag_blocks(q_ref, k_ref, b_ref, a_ref, reverse):
    lane = lax.broadcasted_iota(jnp.int32, (SUB, HEAD_DIM), 1)
    rowi = lax.broadcasted_iota(jnp.int32, (SUB, HEAD_DIM), 0)

    def block(idx, carry):
        h = idx // N_SUB
        i = idx % N_SUB
        r0 = pl.multiple_of(i * SUB, SUB)
        c0 = (N_SUB - 1 - i) * SUB if reverse else r0
        lanes = pl.ds(pl.multiple_of(h * HEAD_DIM, HEAD_DIM), HEAD_DIM)
        qi = q_ref[pl.ds(r0, SUB), lanes].astype(F32)
        ki = k_ref[pl.ds(r0, SUB), lanes].astype(F32)
        bi = b_ref[pl.ds(r0, SUB), lanes]

        def column(s, acc):
            pick = rowi == s
            ks = jnp.sum(jnp.where(pick, ki, 0.0), axis=0, keepdims=True)
            bs = jnp.sum(jnp.where(pick, bi, 0.0), axis=0, keepdims=True)
            w = jnp.exp2(jnp.minimum(bi - bs, 0.0))
            c = jnp.sum(qi * ks * w, axis=-1, keepdims=True)
            return jnp.where(lane == c0 + s, c, acc)

        acc = lax.fori_loop(0, SUB, column, jnp.zeros((SUB, HEAD_DIM), F32))
        keep = (lane - c0 >= rowi) if reverse else (lane - c0 <= rowi)
        in_block = jnp.logical_and(lane >= c0, lane < c0 + SUB)
        old = a_ref[h, pl.ds(r0, SUB), :]
        a_ref[h, pl.ds(r0, SUB), :] = jnp.where(in_block, jnp.where(keep, acc, 0.0), old)
        return carry

    lax.fori_loop(0, HEADS * N_SUB, block, 0)


def _scan_apply(q, k, b, v, a, st_ref, reverse, sub):
    b_last = b[0:1] if reverse else b[CHUNK - 1:CHUNK]
    st = st_ref[...]
    vt = _key_order(v, reverse, sub).T
    lhs = jnp.concatenate([(q.astype(F32) * jnp.exp2(b)).astype(BF16), a.astype(BF16)], axis=1)
    rhs = jnp.concatenate([st.astype(BF16), vt], axis=1)
    k_end = _key_order((k.astype(F32) * jnp.exp2(b_last - b)).astype(BF16), reverse, sub)
    st_ref[...] = st * jnp.exp2(b_last) + _dot(vt, k_end)
    return _dot_nt(lhs, rhs)


def _scan_kernel(qf_ref, kf_ref, bf_ref, vf_ref, qb_ref, kb_ref, bb_ref, vb_ref, s0_ref,
                 of_ref, ob_ref, sout_ref, st_ref, a_ref, *, lat_chunks, lat_nc, ctx_nc):
    g = pl.program_id(0)
    is_lat = g < lat_chunks
    c = jnp.where(is_lat, g % lat_nc, (g - lat_chunks) % ctx_nc)
    last = c == jnp.where(is_lat, lat_nc, ctx_nc) - 1

    @pl.when(jnp.logical_and(c == 0, is_lat))
    def _():
        for d in range(2):
            for h in range(HEADS):
                st_ref[d, h] = s0_ref[0, 0, d, h].T

    @pl.when(jnp.logical_and(c == 0, jnp.logical_not(is_lat)))
    def _():
        st_ref[...] = jnp.zeros_like(st_ref)

    dirs = ((qf_ref, kf_ref, bf_ref, vf_ref, of_ref, False), (qb_ref, kb_ref, bb_ref, vb_ref, ob_ref, True))
    head = lambda h: slice(h * HEAD_DIM, (h + 1) * HEAD_DIM)

    def worst_exponent(sub):
        return jnp.maximum(_max_block_exponent(bf_ref, False, sub), _max_block_exponent(bb_ref, True, sub))

    def chunk(sub):
        for h in range(HEADS):
            for d, (q_ref, k_ref, b_ref, _, _, reverse) in enumerate(dirs):
                a_ref[d, h] = _scan_scores(q_ref[:, head(h)], k_ref[:, head(h)], b_ref[:, head(h)], reverse, sub)

        if sub == SUB:
            @pl.when(worst_exponent(SUB) > FAST_EXP_MAX)
            def _():
                for d, (q_ref, k_ref, b_ref, _, _, reverse) in enumerate(dirs):
                    _exact_diag_blocks(q_ref, k_ref, b_ref, a_ref.at[d], reverse)

        for h in range(HEADS):
            for d, (q_ref, k_ref, b_ref, v_ref, o_ref, reverse) in enumerate(dirs):
                o_ref[:, head(h)] = _scan_apply(q_ref[:, head(h)], k_ref[:, head(h)], b_ref[:, head(h)],
                                                v_ref[:, head(h)], a_ref[d, h], st_ref.at[d, h], reverse, sub)

    wide_ok = worst_exponent(SUB_WIDE) <= FAST_EXP_MAX

    @pl.when(wide_ok)
    def _():
        chunk(SUB_WIDE)

    @pl.when(jnp.logical_not(wide_ok))
    def _():
        chunk(SUB)

    @pl.when(jnp.logical_and(last, jnp.logical_not(is_lat)))
    def _():
        for d in range(2):
            for h in range(HEADS):
                sout_ref[0, d, h] = st_ref[d, h].T


def _scan(q, kf, bf, kb, bb, v, s0, layer, *, lat_b, lat_t, ctx_b, ctx_t):
    n_tok, d = q.shape
    lat_nc, ctx_nc = lat_t // CHUNK, ctx_t // CHUNK
    lat_chunks = lat_b * lat_nc

    def seq_of(g):
        is_lat = g < lat_chunks
        cs = (g - lat_chunks) // ctx_nc
        start = jnp.where(is_lat, (g // lat_nc) * lat_nc, lat_chunks + cs * ctx_nc)
        return start, jnp.where(is_lat, lat_nc, ctx_nc), jnp.where(is_lat, 0, cs)

    def fwd(g):
        return (g, 0)

    def bwd(g):
        start, nc, _ = seq_of(g)
        return (2 * start + nc - 1 - g, 0)

    blk = lambda im: pl.BlockSpec((CHUNK, d), im)
    state_blk = (1, 2, HEADS, HEAD_DIM, HEAD_DIM)
    return pl.pallas_call(
        functools.partial(_scan_kernel, lat_chunks=lat_chunks, lat_nc=lat_nc, ctx_nc=ctx_nc),
        grid=(n_tok // CHUNK,),
        in_specs=[blk(fwd), blk(fwd), blk(fwd), blk(fwd), blk(bwd), blk(bwd), blk(bwd), blk(bwd),
                  pl.BlockSpec((1, 1) + state_blk[1:],
                               lambda g: (jnp.minimum(g // lat_nc, lat_b - 1), layer, 0, 0, 0, 0))],
        out_specs=[blk(fwd), blk(bwd), pl.BlockSpec(state_blk, lambda g: (seq_of(g)[2], 0, 0, 0, 0))],
        out_shape=[jax.ShapeDtypeStruct((n_tok, d), F32), jax.ShapeDtypeStruct((n_tok, d), F32),
                   jax.ShapeDtypeStruct((ctx_b,) + state_blk[1:], F32)],
        scratch_shapes=[pltpu.VMEM((2, HEADS, HEAD_DIM, HEAD_DIM), F32),
                        pltpu.VMEM((2, HEADS, CHUNK, CHUNK), F32)],
        compiler_params=_params("arbitrary"),
        name="hgrn_scan",
    )(q, kf, bf, v, q, kb, bb, v, s0)


def _hgrn_out_mlp_kernel(of_ref, ob_ref, g_ref, x_ref, mod_ref, w_ref, nw_ref, w1_ref, w2_ref, o_ref):
    m = mod_ref[0]
    o = of_ref[...] + ob_ref[...]
    parts = []
    for h in range(HEADS):
        oh = o[:, h * HEAD_DIM:(h + 1) * HEAD_DIM]
        ms = jnp.mean(oh * oh, axis=-1, keepdims=True)
        parts.append(oh * lax.rsqrt(ms + EPS))
    on = jnp.concatenate(parts, axis=1) * g_ref[...].astype(F32)
    x = x_ref[...] + m[2:3] * _dot(on.astype(BF16), w_ref[...])
    o_ref[...] = _mlp_stage(x, m, nw_ref[...], w1_ref, w2_ref)


def _conv_in_kernel(x_ref, mod_ref, nw_ref, w_ref, u_ref):
    d = x_ref.shape[1]
    m = mod_ref[0]
    hn = _norm_mod(x_ref[...], nw_ref[...], m[0:1], m[1:2]).astype(BF16)
    p = _dot(hn, w_ref[...])
    u_ref[...] = p[:, :d] * _sigmoid(p[:, d:])


def _conv_out_kernel(up_ref, u_ref, un_ref, x_ref, mod_ref, wdw_ref, bdw_ref, lng_ref, lnb_ref, w_ref,
                     o_ref, ext_ref, sh_ref, wb_ref, cv_ref, *, n_lat_tiles, tiles_per_lat_seq):
    i = pl.program_id(0)
    tm, d = u_ref.shape
    pos = i % tiles_per_lat_seq
    is_lat = i < n_lat_tiles
    first = jnp.logical_or(jnp.logical_not(is_lat), pos == 0)
    last = jnp.logical_or(jnp.logical_not(is_lat), pos == tiles_per_lat_seq - 1)
    ext_ref[0:HALO, :] = jnp.where(first, 0.0, up_ref[...])
    ext_ref[HALO:HALO + tm, :] = u_ref[...]
    ext_ref[HALO + tm:HALO + tm + HALO, :] = jnp.where(last, 0.0, un_ref[...])

    n_sh = sh_ref.shape[1]
    for s in range(1, SUBLANES):
        sh_ref[s - 1] = ext_ref[s:s + n_sh, :]

    @pl.when(i == 0)
    def _():
        for t in range(CONV_K):
            wb_ref[t] = jnp.broadcast_to(wdw_ref[t:t + 1, :], (SUBLANES, d))

    groups, lanes = 8, 128
    rows = groups * SUBLANES
    off = HALO - CONV_K // 2
    n_a = -(-(CONV_K + off) // SUBLANES)

    def body(r, carry):
        base = pl.multiple_of(r * rows, rows)
        for lc in range(d // lanes):
            ls = slice(lc * lanes, (lc + 1) * lanes)
            accs = [jnp.broadcast_to(bdw_ref[:, ls], (SUBLANES, lanes))] * groups
            for s in range(SUBLANES):
                taps = [(a, a * SUBLANES + s - off) for a in range(n_a)
                        if 0 <= a * SUBLANES + s - off < CONV_K]
                span = rows + max(a for a, _ in taps) * SUBLANES
                if s == 0:
                    blk = ext_ref[pl.ds(base, span), ls]
                else:
                    blk = sh_ref[s - 1, pl.ds(base, span), ls]
                for a, t in taps:
                    w = wb_ref[t, :, ls]
                    accs = [acc + w * blk[(gi + a) * SUBLANES:(gi + a + 1) * SUBLANES]
                            for gi, acc in enumerate(accs)]
            cv_ref[pl.ds(base, rows), ls] = jnp.concatenate(accs, axis=0)
        return carry

    lax.fori_loop(0, tm // rows, body, 0)

    u = cv_ref[...]
    mu = jnp.mean(u, axis=-1, keepdims=True)
    uc = u - mu
    var = jnp.mean(uc * uc, axis=-1, keepdims=True)
    y = uc * lax.rsqrt(var + EPS) * lng_ref[...] + lnb_ref[...]
    out = _dot(_silu(y).astype(BF16), w_ref[...])
    o_ref[...] = x_ref[...] + mod_ref[0][2:3] * out


def _grid_pos_tables(n_tok, dim):
    nf = dim // 4
    omega = 1.0 / (10000.0 ** (jnp.arange(nf, dtype=F32) / nf))
    er = jnp.arange(n_tok // GRID_W, dtype=F32)[:, None] * omega
    ec = jnp.arange(GRID_W, dtype=F32)[:, None] * omega
    return (jnp.concatenate([jnp.sin(er), jnp.cos(er)], axis=-1),
            jnp.concatenate([jnp.sin(ec), jnp.cos(ec)], axis=-1))


def _lower_bounds(p):
    p = jax.nn.softmax(p.astype(F32), axis=0)
    return jnp.maximum(jnp.cumsum(p, axis=0) - p[0], 0.0)


def kernel(x_prompt, x_sample, c, state_hgrn, c_ctx, w_mod, b_mod, norm_mix, norm_mlp, hgrn_w_in, hgrn_lb_fwd, hgrn_lb_bwd, hgrn_g_norm, hgrn_w_out, conv_w_pw1, conv_w_dw, conv_b_dw, conv_ln_g, conv_ln_b, conv_w_pw2, mlp_w1, mlp_w2, final_norm):
    ctx_b, ctx_t, d = x_prompt.shape
    lat_b, lat_t, _ = x_sample.shape
    depth = w_mod.shape[0]
    d_ff = mlp_w1.shape[2]
    n_lat, n_ctx = lat_b * lat_t, ctx_b * ctx_t
    n_tok = n_lat + n_ctx
    assert d == HEADS * HEAD_DIM and lat_b < MOD_ROWS
    assert depth % 2 == 0
    assert lat_t % TM == 0 and n_ctx % TM == 0 and ctx_t % CHUNK == 0 and ctx_t == TM_CONV and d_ff % TF == 0
    assert TM % GRID_W == 0 and (TM // GRID_W) % SUBLANES == 0

    def mod_row(tm):
        return lambda i, *_: (jnp.where(i * tm < n_lat, (i * tm) // lat_t, lat_b), 0, 0)

    def mod_spec(tm):
        return pl.BlockSpec((1, N_MOD, d), mod_row(tm))

    def tok_spec(tm, width=d):
        return pl.BlockSpec((tm, width), lambda i, *_: (i, 0))

    def full_spec(shape):
        return pl.BlockSpec(shape, lambda i, *_: (0,) * len(shape))

    def resident_spec(shape):
        return pl.BlockSpec(shape, lambda i, *_: (0,) * len(shape), pipeline_mode=pl.Buffered(1))

    row = lambda a: a.reshape(1, -1)
    tok_f32 = jax.ShapeDtypeStruct((n_tok, d), F32)
    tok_bf16 = jax.ShapeDtypeStruct((n_tok, d), BF16)

    cvec = jnp.zeros((MOD_ROWS, d), F32).at[:lat_b].set(c).at[lat_b].set(c_ctx)
    mods = _modulation(cvec, w_mod, b_mod).reshape(depth, MOD_ROWS, N_MOD, d)

    lbs_f = _lower_bounds(hgrn_lb_fwd)
    lbs_b = _lower_bounds(hgrn_lb_bwd)

    pos_row, pos_col = _grid_pos_tables(lat_t, d)
    n_lat_tiles = n_lat // TM
    tiles_per_lat_seq = lat_t // TM
    first_specs = [pl.BlockSpec((TM, d), lambda i: (jnp.minimum(i, n_lat_tiles - 1), 0)),
                   pl.BlockSpec((TM // GRID_W, d // 2), lambda i: (i % tiles_per_lat_seq, 0)),
                   full_spec((GRID_W, d // 2)),
                   pl.BlockSpec((TM, d), lambda i: (jnp.maximum(i - n_lat_tiles, 0), 0))]
    x = None

    new_states = []
    for i in range(depth):
        mod = mods[i]
        if i % 2 == 0:
            a = i // 2
            first = i == 0
            x_args = ((x_sample.reshape(n_lat, d), pos_row, pos_col, x_prompt.reshape(n_ctx, d))
                      if first else (x,))
            outs = pl.pallas_call(
                functools.partial(_hgrn_in_kernel, first=first, n_lat_tiles=n_lat_tiles),
                grid=(n_tok // TM,),
                in_specs=(first_specs if first else [tok_spec(TM)]) + [
                    mod_spec(TM), full_spec((1, d)), resident_spec((d, 5 * d)),
                    full_spec((1, d)), full_spec((1, d)), full_spec((1, d))],
                out_specs=[tok_spec(TM)] * (8 if first else 7),
                out_shape=[tok_bf16, tok_bf16, tok_bf16, tok_f32, tok_bf16, tok_f32, tok_bf16]
                + ([tok_f32] if first else []),
                compiler_params=_params("arbitrary"),
                name="hgrn_in",
            )(*x_args, mod, row(norm_mix[i]), hgrn_w_in[a].astype(BF16), row(lbs_f[a]), row(lbs_b[a]),
              row(hgrn_g_norm[a]))
            q, v, kf, bf, kb, bb, g = outs[:7]
            if first:
                x = outs[7]
            of, ob, st = _scan(q, kf, bf, kb, bb, v, state_hgrn, a, lat_b=lat_b, lat_t=lat_t,
                               ctx_b=ctx_b, ctx_t=ctx_t)
            new_states.append(st)
            x = pl.pallas_call(
                _hgrn_out_mlp_kernel,
                grid=(n_tok // TM,),
                in_specs=[tok_spec(TM), tok_spec(TM), tok_spec(TM), tok_spec(TM), mod_spec(TM),
                          full_spec((d, d)), full_spec((1, d)), resident_spec((d, d_ff)),
                          resident_spec((d_ff, d))],
                out_specs=tok_spec(TM),
                out_shape=tok_f32,
                compiler_params=_params("parallel"),
                name="hgrn_out_mlp",
            )(of, ob, g, x, mod, hgrn_w_out[a].astype(BF16), row(norm_mlp[i]), mlp_w1[i].astype(BF16),
              mlp_w2[i].astype(BF16))
            continue

        b = i // 2
        u = pl.pallas_call(
            _conv_in_kernel,
            grid=(n_tok // TM,),
            in_specs=[tok_spec(TM), mod_spec(TM), full_spec((1, d)), full_spec((d, 2 * d))],
            out_specs=tok_spec(TM),
            out_shape=tok_f32,
            compiler_params=_params("parallel"),
            name="conv_in",
        )(x, mod, row(norm_mix[i]), conv_w_pw1[b].astype(BF16))
        tm = TM_CONV
        hb = tm // HALO
        n_halo = n_tok // HALO
        wdw = jnp.zeros((CONV_K + 1, d), F32).at[:CONV_K].set(conv_w_dw[b])
        x = pl.pallas_call(
            functools.partial(_conv_out_kernel, n_lat_tiles=n_lat // tm, tiles_per_lat_seq=lat_t // tm),
            grid=(n_tok // tm,),
            in_specs=[pl.BlockSpec((HALO, d), lambda i: (jnp.maximum(i * hb - 1, 0), 0)),
                      tok_spec(tm),
                      pl.BlockSpec((HALO, d), lambda i: (jnp.minimum((i + 1) * hb, n_halo - 1), 0)),
                      tok_spec(tm), mod_spec(tm), full_spec((CONV_K + 1, d)), full_spec((1, d)),
                      full_spec((1, d)), full_spec((1, d)), full_spec((d, d))],
            out_specs=tok_spec(tm),
            out_shape=tok_f32,
            scratch_shapes=[pltpu.VMEM((tm + 2 * HALO, d), F32),
                            pltpu.VMEM((SUBLANES - 1, tm + 2 * HALO - SUBLANES, d), F32),
                            pltpu.VMEM((CONV_K, SUBLANES, d), F32),
                            pltpu.VMEM((tm, d), F32)],
            compiler_params=_params("arbitrary"),
            name="conv_out",
        )(u, u, u, x, mod, wdw, row(conv_b_dw[b]), row(conv_ln_g[b]), row(conv_ln_b[b]),
          conv_w_pw2[b].astype(BF16))

        final = i == depth - 1
        if final:
            out_specs = [pl.BlockSpec((TM, d), lambda i: (jnp.minimum(i, n_lat_tiles - 1), 0)),
                         pl.BlockSpec((TM, d), lambda i: (jnp.maximum(i - n_lat_tiles, 0), 0))]
            out_shape = [jax.ShapeDtypeStruct((n_lat, d), F32), jax.ShapeDtypeStruct((n_ctx, d), F32)]
        else:
            out_specs, out_shape = tok_spec(TM), tok_f32
        x = pl.pallas_call(
            functools.partial(_mlp_kernel, final=final, n_lat_tiles=n_lat_tiles),
            grid=(n_tok // TM,),
            in_specs=[tok_spec(TM), mod_spec(TM), full_spec((1, d)), full_spec((1, d)),
                      resident_spec((d, d_ff)), resident_spec((d_ff, d))],
            out_specs=out_specs,
            out_shape=out_shape,
            compiler_params=_params("arbitrary"),
            name="mlp",
        )(x, mod, row(norm_mlp[i]), row(final_norm), mlp_w1[i].astype(BF16), mlp_w2[i].astype(BF16))

    y_lat, y_ctx = x
    y_sample = y_lat.reshape(lat_b, lat_t, d)
    y_prompt = y_ctx.reshape(ctx_b, ctx_t, d)
    new_state_hgrn = jnp.stack(new_states, axis=1).astype(x_prompt.dtype)
    return (y_prompt, y_sample, new_state_hgrn)
```

```python
import functools

import jax
import jax.numpy as jnp
from jax import lax
from jax.experimental import pallas as pl
from jax.experimental.pallas import tpu as pltpu

F32 = jnp.float32
BF16 = jnp.bfloat16

N_MOD = 6
HEADS = 8
HEAD_DIM = 128
GRID_W = 64
CONV_K = 31
EPS = 1e-6
K_MAX = 1.0 - 1e-6

CHUNK = 128
SUB = 16
SUB_WIDE = 32
N_SUB = CHUNK // SUB
FAST_EXP_MAX = 108.0
SUBLANES = 8
HALO = 16
MOD_ROWS = 8

TM = 512
TM_CONV = 256
TF = 1024
VMEM_LIMIT = 56 * 1024 * 1024


def _params(*sem):
    return pltpu.CompilerParams(dimension_semantics=sem, vmem_limit_bytes=VMEM_LIMIT)


def _dot(a, b):
    return jnp.dot(a, b, preferred_element_type=F32)


def _dot_nt(a, b):
    return lax.dot_general(a, b, (((1,), (1,)), ((), ())), preferred_element_type=F32)


def _sigmoid(x):
    return 1.0 / (1.0 + jnp.exp(-x))


def _silu(x):
    return x * _sigmoid(x)


def _norm_mod(x, nw, shift, scale):
    ms = jnp.mean(x * x, axis=-1, keepdims=True)
    y = x * lax.rsqrt(ms + EPS) * nw
    return y * (1.0 + scale) + shift


def _mod_kernel(c_ref, w_ref, b_ref, o_ref):
    c = _silu(c_ref[...]).astype(BF16)
    o_ref[0] = _dot(c, w_ref[0].astype(BF16)) + b_ref[0]


def _modulation(cvec, w_mod, b_mod):
    depth, d, n = w_mod.shape
    tn = 1536
    return pl.pallas_call(
        _mod_kernel,
        grid=(depth, n // tn),
        in_specs=[
            pl.BlockSpec((MOD_ROWS, d), lambda l, j: (0, 0)),
            pl.BlockSpec((1, d, tn), lambda l, j: (l, 0, j)),
            pl.BlockSpec((1, 1, tn), lambda l, j: (l, 0, j)),
        ],
        out_specs=pl.BlockSpec((1, MOD_ROWS, tn), lambda l, j: (l, 0, j)),
        out_shape=jax.ShapeDtypeStruct((depth, MOD_ROWS, n), F32),
        compiler_params=_params("parallel", "parallel"),
        name="modulation",
    )(cvec, w_mod, b_mod.reshape(depth, 1, n))


def _mlp_stage(x, m, nw, w1_ref, w2_ref):
    hn = _norm_mod(x, nw, m[3:4], m[4:5]).astype(BF16)
    acc = None
    for j in range(w1_ref.shape[1] // TF):
        h = _dot(hn, w1_ref[:, j * TF:(j + 1) * TF])
        h = jnp.square(jnp.maximum(h, 0.0)).astype(BF16)
        part = _dot(h, w2_ref[j * TF:(j + 1) * TF, :])
        acc = part if acc is None else acc + part
    return x + m[5:6] * acc


def _mlp_kernel(x_ref, mod_ref, nw_ref, fw_ref, w1_ref, w2_ref, *o_refs, final, n_lat_tiles):
    y = _mlp_stage(x_ref[...], mod_ref[0], nw_ref[...], w1_ref, w2_ref)
    if not final:
        o_refs[0][...] = y
        return
    ms = jnp.mean(y * y, axis=-1, keepdims=True)
    y = y * lax.rsqrt(ms + EPS) * fw_ref[...]
    lat_ref, ctx_ref = o_refs
    is_lat = pl.program_id(0) < n_lat_tiles

    @pl.when(is_lat)
    def _():
        lat_ref[...] = y

    @pl.when(jnp.logical_not(is_lat))
    def _():
        ctx_ref[...] = y


def _hgrn_in_kernel(*refs, first, n_lat_tiles):
    if first:
        (xs_ref, prow_ref, pcol_ref, xp_ref, mod_ref, nw_ref, w_ref, lbf_ref, lbb_ref, gn_ref,
         q_ref, v_ref, kf_ref, bf_ref, kb_ref, bb_ref, g_ref, x0_ref) = refs
        is_lat = pl.program_id(0) < n_lat_tiles
        prow, pcol = prow_ref[...], pcol_ref[...]
        n_rows, half = prow.shape
        left = jnp.concatenate([jnp.broadcast_to(prow[r:r + 1], (GRID_W, half)) for r in range(n_rows)], axis=0)
        pos = jnp.concatenate([left, jnp.concatenate([pcol] * n_rows, axis=0)], axis=1)
        x = jnp.where(is_lat, xs_ref[...] + pos, xp_ref[...])
        x0_ref[...] = x
    else:
        (x_ref, mod_ref, nw_ref, w_ref, lbf_ref, lbb_ref, gn_ref,
         q_ref, v_ref, kf_ref, bf_ref, kb_ref, bb_ref, g_ref) = refs
        x = x_ref[...]
    tm, d = x.shape
    m = mod_ref[0]
    hn = _norm_mod(x, nw_ref[...], m[0:1], m[1:2]).astype(BF16)

    def proj(j):
        return _dot(hn, w_ref[:, j * d:(j + 1) * d])

    def gate(j, lb_ref, k_ref):
        k = jnp.minimum((1.0 - lb_ref[...]) * _sigmoid(-proj(j)), K_MAX)
        k_ref[...] = k.astype(BF16)
        lg = jnp.log2(1.0 - k)
        terms = []
        for r in range(tm // CHUNK):
            x = lg[r * CHUNK:(r + 1) * CHUNK]
            hi = x.astype(BF16)
            terms.append(jnp.concatenate([hi, (x - hi.astype(F32)).astype(BF16)], axis=0))
        return terms

    def cumulate(terms, b_ref, reverse):
        row = lax.broadcasted_iota(jnp.int32, (CHUNK, CHUNK), 0)
        col = lax.broadcasted_iota(jnp.int32, (CHUNK, CHUNK), 1)
        tri = jnp.where((col >= row) if reverse else (col <= row), 1.0, 0.0).astype(BF16)
        tri2 = jnp.concatenate([tri, tri], axis=1)
        for r, term in enumerate(terms):
            b_ref[r * CHUNK:(r + 1) * CHUNK, :] = _dot(tri2, term)

    terms_f = gate(2, lbf_ref, kf_ref)
    terms_b = gate(3, lbb_ref, kb_ref)
    cumulate(terms_f, bf_ref, False)
    q_ref[...] = (_silu(proj(0)) * (HEAD_DIM ** -0.5)).astype(BF16)
    cumulate(terms_b, bb_ref, True)
    g_ref[...] = (gn_ref[...] * _silu(proj(4))).astype(BF16)
    v_ref[...] = proj(1).astype(BF16)


def _block_edges(b, reverse, sub):
    n = CHUNK // sub
    edge = 0 if reverse else sub - 1
    far = jnp.concatenate([b[i * sub + edge:i * sub + edge + 1] for i in range(n)], axis=0)
    zero = jnp.zeros_like(far[:1])
    ref = jnp.concatenate([far[1:], zero] if reverse else [zero, far[:-1]], axis=0)
    return far, ref


def _key_order(x, reverse, sub):
    if not reverse:
        return x
    return jnp.concatenate([x[i * sub:(i + 1) * sub] for i in reversed(range(CHUNK // sub))], axis=0)


def _scan_scores(q, k, b, reverse, sub):
    n = CHUNK // sub
    q = q.astype(F32)
    k = k.astype(F32)
    row = lax.broadcasted_iota(jnp.int32, (CHUNK, CHUNK), 0)
    col = lax.broadcasted_iota(jnp.int32, (CHUNK, CHUNK), 1)
    if reverse:
        key = (n - 1 - col // sub) * sub + col % sub
        keep = key >= row
    else:
        keep = col <= row
    far, ref = _block_edges(b, reverse, sub)
    qs, ks = [], []
    for i in range(n):
        rows = slice(i * sub, (i + 1) * sub)
        qs.append((q[rows] * jnp.exp2(b[rows] - ref[i:i + 1])).astype(BF16))
        ks.append(k[rows] * jnp.exp2(far[i:i + 1] - b[rows]))
    blocks = []
    for i in range(n):
        scale = jnp.exp2(ref[i:i + 1] - far)
        sources = range(n - 1, i - 1, -1) if reverse else range(i + 1)
        keys = jnp.concatenate([(ks[j] * scale[j:j + 1]).astype(BF16) for j in sources], axis=0)
        a = _dot_nt(qs[i], keys)
        if len(sources) < n:
            a = jnp.concatenate([a, jnp.zeros((sub, CHUNK - len(sources) * sub), F32)], axis=1)
        blocks.append(a)
    return jnp.where(keep, jnp.concatenate(blocks, axis=0), 0.0)


def _max_block_exponent(b_ref, reverse, sub):
    n = CHUNK // sub
    edge = 0 if reverse else sub - 1
    far = [b_ref[i * sub + edge:i * sub + edge + 1, :] for i in range(n)]
    if reverse:
        worst = -far[n - 1]
        for i in range(n - 1):
            worst = jnp.maximum(worst, far[i + 1] - far[i])
    else:
        worst = -far[0]
        for i in range(1, n):
            worst = jnp.maximum(worst, far[i - 1] - far[i])
    return jnp.max(worst)


def _exact_diag_blocks(q_ref, k_ref, b_ref, a_ref, reverse):
    lane = lax.broadcasted_iota(jnp.int32, (SUB, HEAD_DIM), 1)
    rowi = lax.broadcasted_iota(jnp.int32, (SUB, HEAD_DIM), 0)

    def block(idx, carry):
        h = idx // N_SUB
        i = idx % N_SUB
        r0 = pl.multiple_of(i * SUB, SUB)
        c0 = (N_SUB - 1 - i) * SUB if reverse else r0
        lanes = pl.ds(pl.multiple_of(h * HEAD_DIM, HEAD_DIM), HEAD_DIM)
        qi = q_ref[pl.ds(r0, SUB), lanes].astype(F32)
        ki = k_ref[pl.ds(r0, SUB), lanes].astype(F32)
        bi = b_ref[pl.ds(r0, SUB), lanes]

        def column(s, acc):
            pick = rowi == s
            ks = jnp.sum(jnp.where(pick, ki, 0.0), axis=0, keepdims=True)
            bs = jnp.sum(jnp.where(pick, bi, 0.0), axis=0, keepdims=True)
            w = jnp.exp2(jnp.minimum(bi - bs, 0.0))
            c = jnp.sum(qi * ks * w, axis=-1, keepdims=True)
            return jnp.where(lane == c0 + s, c, acc)

        acc = lax.fori_loop(0, SUB, column, jnp.zeros((SUB, HEAD_DIM), F32))
        keep = (lane - c0 >= rowi) if reverse else (lane - c0 <= rowi)
        in_block = jnp.logical_and(lane >= c0, lane < c0 + SUB)
        old = a_ref[h, pl.ds(r0, SUB), :]
        a_ref[h, pl.ds(r0, SUB), :] = jnp.where(in_block, jnp.where(keep, acc, 0.0), old)
        return carry

    lax.fori_loop(0, HEADS * N_SUB, block, 0)


def _scan_apply(q, k, b, v, a, st_ref, reverse, sub):
    b_last = b[0:1] if reverse else b[CHUNK - 1:CHUNK]
    st = st_ref[...]
    vt = _key_order(v, reverse, sub).T
    lhs = jnp.concatenate([(q.astype(F32) * jnp.exp2(b)).astype(BF16), a.astype(BF16)], axis=1)
    rhs = jnp.concatenate([st.astype(BF16), vt], axis=1)
    k_end = _key_order((k.astype(F32) * jnp.exp2(b_last - b)).astype(BF16), reverse, sub)
    st_ref[...] = st * jnp.exp2(b_last) + _dot(vt, k_end)
    return _dot_nt(lhs, rhs)


def _scan_kernel(*refs, layer, lat_chunks, lat_nc, ctx_nc):
    qf_ref, kf_ref, bf_ref, vf_ref, qb_ref, kb_ref, bb_ref, vb_ref, s0_ref = refs[:9]
    prev_ref = refs[9] if layer > 0 else None
    of_ref, ob_ref, sout_ref, st_ref, a_ref = refs[-5:]
    g = pl.program_id(0)
    is_lat = g < lat_chunks
    c = jnp.where(is_lat, g % lat_nc, (g - lat_chunks) % ctx_nc)
    last = c == jnp.where(is_lat, lat_nc, ctx_nc) - 1

    @pl.when(jnp.logical_and(c == 0, is_lat))
    def _():
        for d in range(2):
            for h in range(HEADS):
                st_ref[d, h] = s0_ref[0, 0, d, h].T

    @pl.when(jnp.logical_and(c == 0, jnp.logical_not(is_lat)))
    def _():
        st_ref[...] = jnp.zeros_like(st_ref)

    dirs = ((qf_ref, kf_ref, bf_ref, vf_ref, of_ref, False), (qb_ref, kb_ref, bb_ref, vb_ref, ob_ref, True))
    head = lambda h: slice(h * HEAD_DIM, (h + 1) * HEAD_DIM)

    def worst_exponent(sub):
        return jnp.maximum(_max_block_exponent(bf_ref, False, sub), _max_block_exponent(bb_ref, True, sub))

    def chunk(sub):
        for h in range(HEADS):
            for d, (q_ref, k_ref, b_ref, _, _, reverse) in enumerate(dirs):
                a_ref[d, h] = _scan_scores(q_ref[:, head(h)], k_ref[:, head(h)], b_ref[:, head(h)], reverse, sub)

        if sub == SUB:
            @pl.when(worst_exponent(SUB) > FAST_EXP_MAX)
            def _():
                for d, (q_ref, k_ref, b_ref, _, _, reverse) in enumerate(dirs):
                    _exact_diag_blocks(q_ref, k_ref, b_ref, a_ref.at[d], reverse)

        for h in range(HEADS):
            for d, (q_ref, k_ref, b_ref, v_ref, o_ref, reverse) in enumerate(dirs):
                o_ref[:, head(h)] = _scan_apply(q_ref[:, head(h)], k_ref[:, head(h)], b_ref[:, head(h)],
                                                v_ref[:, head(h)], a_ref[d, h], st_ref.at[d, h], reverse, sub)

    wide_ok = worst_exponent(SUB_WIDE) <= FAST_EXP_MAX

    @pl.when(wide_ok)
    def _():
        chunk(SUB_WIDE)

    @pl.when(jnp.logical_not(wide_ok))
    def _():
        chunk(SUB)

    @pl.when(jnp.logical_and(last, jnp.logical_not(is_lat)))
    def _():
        for l in range(layer):
            sout_ref[0, l] = prev_ref[0, l]
        for d in range(2):
            for h in range(HEADS):
                sout_ref[0, layer, d, h] = st_ref[d, h].T


def _scan(q, kf, bf, kb, bb, v, s0, prev_states, layer, *, lat_b, lat_t, ctx_b, ctx_t):
    n_tok, d = q.shape
    lat_nc, ctx_nc = lat_t // CHUNK, ctx_t // CHUNK
    lat_chunks = lat_b * lat_nc

    def seq_of(g):
        is_lat = g < lat_chunks
        cs = (g - lat_chunks) // ctx_nc
        start = jnp.where(is_lat, (g // lat_nc) * lat_nc, lat_chunks + cs * ctx_nc)
        return start, jnp.where(is_lat, lat_nc, ctx_nc), jnp.where(is_lat, 0, cs)

    def fwd(g):
        return (g, 0)

    def bwd(g):
        start, nc, _ = seq_of(g)
        return (2 * start + nc - 1 - g, 0)

    blk = lambda im: pl.BlockSpec((CHUNK, d), im)
    state = (2, HEADS, HEAD_DIM, HEAD_DIM)
    ctx_states = lambda n: pl.BlockSpec((1, n) + state, lambda g: (seq_of(g)[2], 0, 0, 0, 0, 0))
    in_specs = [blk(fwd), blk(fwd), blk(fwd), blk(fwd), blk(bwd), blk(bwd), blk(bwd), blk(bwd),
                pl.BlockSpec((1, 1) + state, lambda g: (jnp.minimum(g // lat_nc, lat_b - 1), layer, 0, 0, 0, 0))]
    args = [q, kf, bf, v, q, kb, bb, v, s0]
    if layer > 0:
        in_specs.append(ctx_states(layer))
        args.append(prev_states)
    return pl.pallas_call(
        functools.partial(_scan_kernel, layer=layer, lat_chunks=lat_chunks, lat_nc=lat_nc, ctx_nc=ctx_nc),
        grid=(n_tok // CHUNK,),
        in_specs=in_specs,
        out_specs=[blk(fwd), blk(bwd), ctx_states(layer + 1)],
        out_shape=[jax.ShapeDtypeStruct((n_tok, d), F32), jax.ShapeDtypeStruct((n_tok, d), F32),
                   jax.ShapeDtypeStruct((ctx_b, layer + 1) + state, F32)],
        scratch_shapes=[pltpu.VMEM((2, HEADS, HEAD_DIM, HEAD_DIM), F32),
                        pltpu.VMEM((2, HEADS, CHUNK, CHUNK), F32)],
        compiler_params=_params("arbitrary"),
        name="hgrn_scan",
    )(*args)


def _hgrn_out_mlp_kernel(of_ref, ob_ref, g_ref, x_ref, mod_ref, w_ref, nw_ref, w1_ref, w2_ref, o_ref):
    m = mod_ref[0]
    o = of_ref[...] + ob_ref[...]
    parts = []
    for h in range(HEADS):
        oh = o[:, h * HEAD_DIM:(h + 1) * HEAD_DIM]
        ms = jnp.mean(oh * oh, axis=-1, keepdims=True)
        parts.append(oh * lax.rsqrt(ms + EPS))
    on = jnp.concatenate(parts, axis=1) * g_ref[...].astype(F32)
    x = x_ref[...] + m[2:3] * _dot(on.astype(BF16), w_ref[...])
    o_ref[...] = _mlp_stage(x, m, nw_ref[...], w1_ref, w2_ref)


def _conv_in_kernel(x_ref, mod_ref, nw_ref, w_ref, u_ref):
    d = x_ref.shape[1]
    m = mod_ref[0]
    hn = _norm_mod(x_ref[...], nw_ref[...], m[0:1], m[1:2]).astype(BF16)
    p = _dot(hn, w_ref[...])
    u_ref[...] = p[:, :d] * _sigmoid(p[:, d:])


def _conv_out_kernel(up_ref, u_ref, un_ref, x_ref, mod_ref, wdw_ref, bdw_ref, lng_ref, lnb_ref, w_ref,
                     o_ref, ext_ref, sh_ref, wb_ref, cv_ref, *, n_lat_tiles, tiles_per_lat_seq):
    i = pl.program_id(0)
    tm, d = u_ref.shape
    pos = i % tiles_per_lat_seq
    is_lat = i < n_lat_tiles
    first = jnp.logical_or(jnp.logical_not(is_lat), pos == 0)
    last = jnp.logical_or(jnp.logical_not(is_lat), pos == tiles_per_lat_seq - 1)
    ext_ref[0:HALO, :] = jnp.where(first, 0.0, up_ref[...])
    ext_ref[HALO:HALO + tm, :] = u_ref[...]
    ext_ref[HALO + tm:HALO + tm + HALO, :] = jnp.where(last, 0.0, un_ref[...])

    n_sh = sh_ref.shape[1]
    for s in range(1, SUBLANES):
        sh_ref[s - 1] = ext_ref[s:s + n_sh, :]

    @pl.when(i == 0)
    def _():
        for t in range(CONV_K):
            wb_ref[t] = jnp.broadcast_to(wdw_ref[t:t + 1, :], (SUBLANES, d))

    groups, lanes = 8, 128
    rows = groups * SUBLANES
    off = HALO - CONV_K // 2
    n_a = -(-(CONV_K + off) // SUBLANES)

    def body(r, carry):
        base = pl.multiple_of(r * rows, rows)
        for lc in range(d // lanes):
            ls = slice(lc * lanes, (lc + 1) * lanes)
            accs = [jnp.broadcast_to(bdw_ref[:, ls], (SUBLANES, lanes))] * groups
            for s in range(SUBLANES):
                taps = [(a, a * SUBLANES + s - off) for a in range(n_a)
                        if 0 <= a * SUBLANES + s - off < CONV_K]
                span = rows + max(a for a, _ in taps) * SUBLANES
                if s == 0:
                    blk = ext_ref[pl.ds(base, span), ls]
                else:
                    blk = sh_ref[s - 1, pl.ds(base, span), ls]
                for a, t in taps:
                    w = wb_ref[t, :, ls]
                    accs = [acc + w * blk[(gi + a) * SUBLANES:(gi + a + 1) * SUBLANES]
                            for gi, acc in enumerate(accs)]
            cv_ref[pl.ds(base, rows), ls] = jnp.concatenate(accs, axis=0)
        return carry

    lax.fori_loop(0, tm // rows, body, 0)

    u = cv_ref[...]
    mu = jnp.mean(u, axis=-1, keepdims=True)
    uc = u - mu
    var = jnp.mean(uc * uc, axis=-1, keepdims=True)
    y = uc * lax.rsqrt(var + EPS) * lng_ref[...] + lnb_ref[...]
    out = _dot(_silu(y).astype(BF16), w_ref[...])
    o_ref[...] = x_ref[...] + mod_ref[0][2:3] * out


def _grid_pos_tables(n_tok, dim):
    nf = dim // 4
    omega = 1.0 / (10000.0 ** (jnp.arange(nf, dtype=F32) / nf))
    er = jnp.arange(n_tok // GRID_W, dtype=F32)[:, None] * omega
    ec = jnp.arange(GRID_W, dtype=F32)[:, None] * omega
    return (jnp.concatenate([jnp.sin(er), jnp.cos(er)], axis=-1),
            jnp.concatenate([jnp.sin(ec), jnp.cos(ec)], axis=-1))


def _lower_bounds(p):
    p = jax.nn.softmax(p.astype(F32), axis=0)
    return jnp.maximum(jnp.cumsum(p, axis=0) - p[0], 0.0)


def kernel(x_prompt, x_sample, c, state_hgrn, c_ctx, w_mod, b_mod, norm_mix, norm_mlp, hgrn_w_in, hgrn_lb_fwd, hgrn_lb_bwd, hgrn_g_norm, hgrn_w_out, conv_w_pw1, conv_w_dw, conv_b_dw, conv_ln_g, conv_ln_b, conv_w_pw2, mlp_w1, mlp_w2, final_norm):
    ctx_b, ctx_t, d = x_prompt.shape
    lat_b, lat_t, _ = x_sample.shape
    depth = w_mod.shape[0]
    d_ff = mlp_w1.shape[2]
    n_lat, n_ctx = lat_b * lat_t, ctx_b * ctx_t
    n_tok = n_lat + n_ctx
    assert d == HEADS * HEAD_DIM and lat_b < MOD_ROWS
    assert depth % 2 == 0
    assert lat_t % TM == 0 and n_ctx % TM == 0 and ctx_t % CHUNK == 0 and ctx_t == TM_CONV and d_ff % TF == 0
    assert TM % GRID_W == 0 and (TM // GRID_W) % SUBLANES == 0

    def mod_row(tm):
        return lambda i, *_: (jnp.where(i * tm < n_lat, (i * tm) // lat_t, lat_b), 0, 0)

    def mod_spec(tm):
        return pl.BlockSpec((1, N_MOD, d), mod_row(tm))

    def tok_spec(tm, width=d):
        return pl.BlockSpec((tm, width), lambda i, *_: (i, 0))

    def full_spec(shape):
        return pl.BlockSpec(shape, lambda i, *_: (0,) * len(shape))

    def resident_spec(shape):
        return pl.BlockSpec(shape, lambda i, *_: (0,) * len(shape), pipeline_mode=pl.Buffered(1))

    row = lambda a: a.reshape(1, -1)
    tok_f32 = jax.ShapeDtypeStruct((n_tok, d), F32)
    tok_bf16 = jax.ShapeDtypeStruct((n_tok, d), BF16)

    cvec = jnp.zeros((MOD_ROWS, d), F32).at[:lat_b].set(c).at[lat_b].set(c_ctx)
    mods = _modulation(cvec, w_mod, b_mod).reshape(depth, MOD_ROWS, N_MOD, d)

    lbs_f = _lower_bounds(hgrn_lb_fwd)
    lbs_b = _lower_bounds(hgrn_lb_bwd)

    pos_row, pos_col = _grid_pos_tables(lat_t, d)
    n_lat_tiles = n_lat // TM
    tiles_per_lat_seq = lat_t // TM
    first_specs = [pl.BlockSpec((TM, d), lambda i: (jnp.minimum(i, n_lat_tiles - 1), 0)),
                   pl.BlockSpec((TM // GRID_W, d // 2), lambda i: (i % tiles_per_lat_seq, 0)),
                   full_spec((GRID_W, d // 2)),
                   pl.BlockSpec((TM, d), lambda i: (jnp.maximum(i - n_lat_tiles, 0), 0))]
    x = None

    new_states = None
    for i in range(depth):
        mod = mods[i]
        if i % 2 == 0:
            a = i // 2
            first = i == 0
            x_args = ((x_sample.reshape(n_lat, d), pos_row, pos_col, x_prompt.reshape(n_ctx, d))
                      if first else (x,))
            outs = pl.pallas_call(
                functools.partial(_hgrn_in_kernel, first=first, n_lat_tiles=n_lat_tiles),
                grid=(n_tok // TM,),
                in_specs=(first_specs if first else [tok_spec(TM)]) + [
                    mod_spec(TM), full_spec((1, d)), resident_spec((d, 5 * d)),
                    full_spec((1, d)), full_spec((1, d)), full_spec((1, d))],
                out_specs=[tok_spec(TM)] * (8 if first else 7),
                out_shape=[tok_bf16, tok_bf16, tok_bf16, tok_f32, tok_bf16, tok_f32, tok_bf16]
                + ([tok_f32] if first else []),
                compiler_params=_params("arbitrary"),
                name="hgrn_in",
            )(*x_args, mod, row(norm_mix[i]), hgrn_w_in[a].astype(BF16), row(lbs_f[a]), row(lbs_b[a]),
              row(hgrn_g_norm[a]))
            q, v, kf, bf, kb, bb, g = outs[:7]
            if first:
                x = outs[7]
            of, ob, new_states = _scan(q, kf, bf, kb, bb, v, state_hgrn, new_states, a, lat_b=lat_b, lat_t=lat_t,
                                       ctx_b=ctx_b, ctx_t=ctx_t)
            x = pl.pallas_call(
                _hgrn_out_mlp_kernel,
                grid=(n_tok // TM,),
                in_specs=[tok_spec(TM), tok_spec(TM), tok_spec(TM), tok_spec(TM), mod_spec(TM),
                          full_spec((d, d)), full_spec((1, d)), resident_spec((d, d_ff)),
                          resident_spec((d_ff, d))],
                out_specs=tok_spec(TM),
                out_shape=tok_f32,
                compiler_params=_params("parallel"),
                name="hgrn_out_mlp",
            )(of, ob, g, x, mod, hgrn_w_out[a].astype(BF16), row(norm_mlp[i]), mlp_w1[i].astype(BF16),
              mlp_w2[i].astype(BF16))
            continue

        b = i // 2
        u = pl.pallas_call(
            _conv_in_kernel,
            grid=(n_tok // TM,),
            in_specs=[tok_spec(TM), mod_spec(TM), full_spec((1, d)), full_spec((d, 2 * d))],
            out_specs=tok_spec(TM),
            out_shape=tok_f32,
            compiler_params=_params("parallel"),
            name="conv_in",
        )(x, mod, row(norm_mix[i]), conv_w_pw1[b].astype(BF16))
        tm = TM_CONV
        hb = tm // HALO
        n_halo = n_tok // HALO
        wdw = jnp.zeros((CONV_K + 1, d), F32).at[:CONV_K].set(conv_w_dw[b])
        x = pl.pallas_call(
            functools.partial(_conv_out_kernel, n_lat_tiles=n_lat // tm, tiles_per_lat_seq=lat_t // tm),
            grid=(n_tok // tm,),
            in_specs=[pl.BlockSpec((HALO, d), lambda i: (jnp.maximum(i * hb - 1, 0), 0)),
                      tok_spec(tm),
                      pl.BlockSpec((HALO, d), lambda i: (jnp.minimum((i + 1) * hb, n_halo - 1), 0)),
                      tok_spec(tm), mod_spec(tm), full_spec((CONV_K + 1, d)), full_spec((1, d)),
                      full_spec((1, d)), full_spec((1, d)), full_spec((d, d))],
            out_specs=tok_spec(tm),
            out_shape=tok_f32,
            scratch_shapes=[pltpu.VMEM((tm + 2 * HALO, d), F32),
                            pltpu.VMEM((SUBLANES - 1, tm + 2 * HALO - SUBLANES, d), F32),
                            pltpu.VMEM((CONV_K, SUBLANES, d), F32),
                            pltpu.VMEM((tm, d), F32)],
            compiler_params=_params("arbitrary"),
            name="conv_out",
        )(u, u, u, x, mod, wdw, row(conv_b_dw[b]), row(conv_ln_g[b]), row(conv_ln_b[b]),
          conv_w_pw2[b].astype(BF16))

        final = i == depth - 1
        if final:
            out_specs = [pl.BlockSpec((TM, d), lambda i: (jnp.minimum(i, n_lat_tiles - 1), 0)),
                         pl.BlockSpec((TM, d), lambda i: (jnp.maximum(i - n_lat_tiles, 0), 0))]
            out_shape = [jax.ShapeDtypeStruct((n_lat, d), F32), jax.ShapeDtypeStruct((n_ctx, d), F32)]
        else:
            out_specs, out_shape = tok_spec(TM), tok_f32
        x = pl.pallas_call(
            functools.partial(_mlp_kernel, final=final, n_lat_tiles=n_lat_tiles),
            grid=(n_tok // TM,),
            in_specs=[tok_spec(TM), mod_spec(TM), full_spec((1, d)), full_spec((1, d)),
                      resident_spec((d, d_ff)), resident_spec((d_ff, d))],
            out_specs=out_specs,
            out_shape=out_shape,
            compiler_params=_params("arbitrary"),
            name="mlp",
        )(x, mod, row(norm_mlp[i]), row(final_norm), mlp_w1[i].astype(BF16), mlp_w2[i].astype(BF16))

    y_lat, y_ctx = x
    y_sample = y_lat.reshape(lat_b, lat_t, d)
    y_prompt = y_ctx.reshape(ctx_b, ctx_t, d)
    return (y_prompt, y_sample, new_states.astype(x_prompt.dtype))
```

```python
import functools

import jax
import jax.numpy as jnp
from jax import lax
from jax.experimental import pallas as pl
from jax.experimental.pallas import tpu as pltpu

F32 = jnp.float32
BF16 = jnp.bfloat16

N_MOD = 6
HEADS = 8
HEAD_DIM = 128
GRID_W = 64
CONV_K = 31
EPS = 1e-6
K_MAX = 1.0 - 1e-6

CHUNK = 128
SUB = 16
SUB_WIDE = 32
N_SUB = CHUNK // SUB
FAST_EXP_MAX = 108.0
SUBLANES = 8
HALO = 16
MOD_ROWS = 8

TM = 512
TM_CONV = 256
TF = 1024
GLU_COLS = 256
K_SLICE = 256
VMEM_LIMIT = 56 * 1024 * 1024


def _params(*sem):
    return pltpu.CompilerParams(dimension_semantics=sem, vmem_limit_bytes=VMEM_LIMIT)


def _dot(a, b):
    return jnp.dot(a, b, preferred_element_type=F32)


def _dot_nt(a, b):
    return lax.dot_general(a, b, (((1,), (1,)), ((), ())), preferred_element_type=F32)


def _sigmoid(x):
    return 1.0 / (1.0 + jnp.exp(-x))


def _silu(x):
    return x * _sigmoid(x)


def _norm_mod(x, nw, shift, scale):
    ms = jnp.mean(x * x, axis=-1, keepdims=True)
    y = x * lax.rsqrt(ms + EPS) * nw
    return y * (1.0 + scale) + shift


def _mod_kernel(c_ref, w_ref, b_ref, o_ref):
    c = _silu(c_ref[...]).astype(BF16)
    o_ref[0] = _dot(c, w_ref[0].astype(BF16)) + b_ref[0]


def _modulation(cvec, w_mod, b_mod):
    depth, d, n = w_mod.shape
    tn = 1536
    return pl.pallas_call(
        _mod_kernel,
        grid=(depth, n // tn),
        in_specs=[
            pl.BlockSpec((MOD_ROWS, d), lambda l, j: (0, 0)),
            pl.BlockSpec((1, d, tn), lambda l, j: (l, 0, j)),
            pl.BlockSpec((1, 1, tn), lambda l, j: (l, 0, j)),
        ],
        out_specs=pl.BlockSpec((1, MOD_ROWS, tn), lambda l, j: (l, 0, j)),
        out_shape=jax.ShapeDtypeStruct((depth, MOD_ROWS, n), F32),
        compiler_params=_params("parallel", "parallel"),
        name="modulation",
    )(cvec, w_mod, b_mod.reshape(depth, 1, n))


def _mlp_stage(x, m, nw, w1_ref, w2_ref):
    hn = _norm_mod(x, nw, m[3:4], m[4:5]).astype(BF16)
    acc = None
    for j in range(w1_ref.shape[1] // TF):
        h = _dot(hn, w1_ref[:, j * TF:(j + 1) * TF])
        h = jnp.square(jnp.maximum(h, 0.0)).astype(BF16)
        part = _dot(h, w2_ref[j * TF:(j + 1) * TF, :])
        acc = part if acc is None else acc + part
    return x + m[5:6] * acc


def _mlp_kernel(x_ref, mod_ref, nw_ref, fw_ref, w1_ref, w2_ref, *o_refs, final, n_lat_tiles):
    y = _mlp_stage(x_ref[...], mod_ref[0], nw_ref[...], w1_ref, w2_ref)
    if not final:
        o_refs[0][...] = y
        return
    ms = jnp.mean(y * y, axis=-1, keepdims=True)
    y = y * lax.rsqrt(ms + EPS) * fw_ref[...]
    lat_ref, ctx_ref = o_refs
    is_lat = pl.program_id(0) < n_lat_tiles

    @pl.when(is_lat)
    def _():
        lat_ref[...] = y

    @pl.when(jnp.logical_not(is_lat))
    def _():
        ctx_ref[...] = y


def _hgrn_in_kernel(*refs, first, n_lat_tiles):
    if first:
        (xs_ref, prow_ref, pcol_ref, xp_ref, mod_ref, nw_ref, w_ref, lbf_ref, lbb_ref, gn_ref,
         q_ref, v_ref, kf_ref, bf_ref, kb_ref, bb_ref, g_ref, x0_ref) = refs
        is_lat = pl.program_id(0) < n_lat_tiles
        prow, pcol = prow_ref[...], pcol_ref[...]
        n_rows, half = prow.shape
        left = jnp.concatenate([jnp.broadcast_to(prow[r:r + 1], (GRID_W, half)) for r in range(n_rows)], axis=0)
        pos = jnp.concatenate([left, jnp.concatenate([pcol] * n_rows, axis=0)], axis=1)
        x = jnp.where(is_lat, xs_ref[...] + pos, xp_ref[...])
        x0_ref[...] = x
    else:
        (x_ref, mod_ref, nw_ref, w_ref, lbf_ref, lbb_ref, gn_ref,
         q_ref, v_ref, kf_ref, bf_ref, kb_ref, bb_ref, g_ref) = refs
        x = x_ref[...]
    tm, d = x.shape
    m = mod_ref[0]
    hn = _norm_mod(x, nw_ref[...], m[0:1], m[1:2]).astype(BF16)

    def proj(j):
        return _dot(hn, w_ref[:, j * d:(j + 1) * d])

    def gate(j, lb_ref, k_ref):
        k = jnp.minimum((1.0 - lb_ref[...]) * _sigmoid(-proj(j)), K_MAX)
        k_ref[...] = k.astype(BF16)
        lg = jnp.log2(1.0 - k)
        terms = []
        for r in range(tm // CHUNK):
            x = lg[r * CHUNK:(r + 1) * CHUNK]
            hi = x.astype(BF16)
            terms.append(jnp.concatenate([hi, (x - hi.astype(F32)).astype(BF16)], axis=0))
        return terms

    def cumulate(terms, b_ref, reverse):
        row = lax.broadcasted_iota(jnp.int32, (CHUNK, CHUNK), 0)
        col = lax.broadcasted_iota(jnp.int32, (CHUNK, CHUNK), 1)
        tri = jnp.where((col >= row) if reverse else (col <= row), 1.0, 0.0).astype(BF16)
        tri2 = jnp.concatenate([tri, tri], axis=1)
        for r, term in enumerate(terms):
            b_ref[r * CHUNK:(r + 1) * CHUNK, :] = _dot(tri2, term)

    terms_f = gate(2, lbf_ref, kf_ref)
    terms_b = gate(3, lbb_ref, kb_ref)
    cumulate(terms_f, bf_ref, False)
    q_ref[...] = (_silu(proj(0)) * (HEAD_DIM ** -0.5)).astype(BF16)
    cumulate(terms_b, bb_ref, True)
    g_ref[...] = (gn_ref[...] * _silu(proj(4))).astype(BF16)
    v_ref[...] = proj(1).astype(BF16)


def _block_edges(b, reverse, sub):
    n = CHUNK // sub
    edge = 0 if reverse else sub - 1
    far = jnp.concatenate([b[i * sub + edge:i * sub + edge + 1] for i in range(n)], axis=0)
    zero = jnp.zeros_like(far[:1])
    ref = jnp.concatenate([far[1:], zero] if reverse else [zero, far[:-1]], axis=0)
    return far, ref


def _key_order(x, reverse, sub):
    if not reverse:
        return x
    return jnp.concatenate([x[i * sub:(i + 1) * sub] for i in reversed(range(CHUNK // sub))], axis=0)


def _scan_scores(q, k, b, reverse, sub):
    n = CHUNK // sub
    q = q.astype(F32)
    k = k.astype(F32)
    row = lax.broadcasted_iota(jnp.int32, (CHUNK, CHUNK), 0)
    col = lax.broadcasted_iota(jnp.int32, (CHUNK, CHUNK), 1)
    if reverse:
        key = (n - 1 - col // sub) * sub + col % sub
        keep = key >= row
    else:
        keep = col <= row
    far, ref = _block_edges(b, reverse, sub)
    qs, ks = [], []
    for i in range(n):
        rows = slice(i * sub, (i + 1) * sub)
        qs.append((q[rows] * jnp.exp2(b[rows] - ref[i:i + 1])).astype(BF16))
        ks.append(k[rows] * jnp.exp2(far[i:i + 1] - b[rows]))
    blocks = []
    for i in range(n):
        scale = jnp.exp2(ref[i:i + 1] - far)
        sources = range(n - 1, i - 1, -1) if reverse else range(i + 1)
        keys = jnp.concatenate([(ks[j] * scale[j:j + 1]).astype(BF16) for j in sources], axis=0)
        a = _dot_nt(qs[i], keys)
        if len(sources) < n:
            a = jnp.concatenate([a, jnp.zeros((sub, CHUNK - len(sources) * sub), F32)], axis=1)
        blocks.append(a)
    return jnp.where(keep, jnp.concatenate(blocks, axis=0), 0.0)


def _max_block_exponent(b_ref, reverse, sub):
    n = CHUNK // sub
    edge = 0 if reverse else sub - 1
    far = [b_ref[i * sub + edge:i * sub + edge + 1, :] for i in range(n)]
    if reverse:
        worst = -far[n - 1]
        for i in range(n - 1):
            worst = jnp.maximum(worst, far[i + 1] - far[i])
    else:
        worst = -far[0]
        for i in range(1, n):
            worst = jnp.maximum(worst, far[i - 1] - far[i])
    return jnp.max(worst)


def _exact_diag_blocks(q_ref, k_ref, b_ref, a_ref, reverse):
    lane = lax.broadcasted_iota(jnp.int32, (SUB, HEAD_DIM), 1)
    rowi = lax.broadcasted_iota(jnp.int32, (SUB, HEAD_DIM), 0)

    def block(idx, carry):
        h = idx // N_SUB
        i = idx % N_SUB
        r0 = pl.multiple_of(i * SUB, SUB)
        c0 = (N_SUB - 1 - i) * SUB if reverse else r0
        lanes = pl.ds(pl.multiple_of(h * HEAD_DIM, HEAD_DIM), HEAD_DIM)
        qi = q_ref[pl.ds(r0, SUB), lanes].astype(F32)
        ki = k_ref[pl.ds(r0, SUB), lanes].astype(F32)
        bi = b_ref[pl.ds(r0, SUB), lanes]

        def column(s, acc):
            pick = rowi == s
            ks = jnp.sum(jnp.where(pick, ki, 0.0), axis=0, keepdims=True)
            bs = jnp.sum(jnp.where(pick, bi, 0.0), axis=0, keepdims=True)
            w = jnp.exp2(jnp.minimum(bi - bs, 0.0))
            c = jnp.sum(qi * ks * w, axis=-1, keepdims=True)
            return jnp.where(lane == c0 + s, c, acc)

        acc = lax.fori_loop(0, SUB, column, jnp.zeros((SUB, HEAD_DIM), F32))
        keep = (lane - c0 >= rowi) if reverse else (lane - c0 <= rowi)
        in_block = jnp.logical_and(lane >= c0, lane < c0 + SUB)
        old = a_ref[h, pl.ds(r0, SUB), :]
        a_ref[h, pl.ds(r0, SUB), :] = jnp.where(in_block, jnp.where(keep, acc, 0.0), old)
        return carry

    lax.fori_loop(0, HEADS * N_SUB, block, 0)


def _scan_apply(q, k, b, v, a, st_ref, reverse, sub):
    b_last = b[0:1] if reverse else b[CHUNK - 1:CHUNK]
    st = st_ref[...]
    vt = _key_order(v, reverse, sub).T
    lhs = jnp.concatenate([(q.astype(F32) * jnp.exp2(b)).astype(BF16), a.astype(BF16)], axis=1)
    rhs = jnp.concatenate([st.astype(BF16), vt], axis=1)
    k_end = _key_order((k.astype(F32) * jnp.exp2(b_last - b)).astype(BF16), reverse, sub)
    st_ref[...] = st * jnp.exp2(b_last) + _dot(vt, k_end)
    return _dot_nt(lhs, rhs)


def _scan_kernel(*refs, layer, lat_chunks, lat_nc, ctx_nc):
    qf_ref, kf_ref, bf_ref, vf_ref, qb_ref, kb_ref, bb_ref, vb_ref, s0_ref = refs[:9]
    prev_ref = refs[9] if layer > 0 else None
    of_ref, ob_ref, sout_ref, st_ref, a_ref = refs[-5:]
    g = pl.program_id(0)
    is_lat = g < lat_chunks
    c = jnp.where(is_lat, g % lat_nc, (g - lat_chunks) % ctx_nc)
    last = c == jnp.where(is_lat, lat_nc, ctx_nc) - 1

    @pl.when(jnp.logical_and(c == 0, is_lat))
    def _():
        for d in range(2):
            for h in range(HEADS):
                st_ref[d, h] = s0_ref[0, 0, d, h].T

    @pl.when(jnp.logical_and(c == 0, jnp.logical_not(is_lat)))
    def _():
        st_ref[...] = jnp.zeros_like(st_ref)

    dirs = ((qf_ref, kf_ref, bf_ref, vf_ref, of_ref, False), (qb_ref, kb_ref, bb_ref, vb_ref, ob_ref, True))
    head = lambda h: slice(h * HEAD_DIM, (h + 1) * HEAD_DIM)

    def worst_exponent(sub):
        return jnp.maximum(_max_block_exponent(bf_ref, False, sub), _max_block_exponent(bb_ref, True, sub))

    def chunk(sub):
        for h in range(HEADS):
            for d, (q_ref, k_ref, b_ref, _, _, reverse) in enumerate(dirs):
                a_ref[d, h] = _scan_scores(q_ref[:, head(h)], k_ref[:, head(h)], b_ref[:, head(h)], reverse, sub)

        if sub == SUB:
            @pl.when(worst_exponent(SUB) > FAST_EXP_MAX)
            def _():
                for d, (q_ref, k_ref, b_ref, _, _, reverse) in enumerate(dirs):
                    _exact_diag_blocks(q_ref, k_ref, b_ref, a_ref.at[d], reverse)

        for h in range(HEADS):
            for d, (q_ref, k_ref, b_ref, v_ref, o_ref, reverse) in enumerate(dirs):
                o_ref[:, head(h)] = _scan_apply(q_ref[:, head(h)], k_ref[:, head(h)], b_ref[:, head(h)],
                                                v_ref[:, head(h)], a_ref[d, h], st_ref.at[d, h], reverse, sub)

    wide_ok = worst_exponent(SUB_WIDE) <= FAST_EXP_MAX

    @pl.when(wide_ok)
    def _():
        chunk(SUB_WIDE)

    @pl.when(jnp.logical_not(wide_ok))
    def _():
        chunk(SUB)

    @pl.when(jnp.logical_and(last, jnp.logical_not(is_lat)))
    def _():
        for l in range(layer):
            sout_ref[0, l] = prev_ref[0, l]
        for d in range(2):
            for h in range(HEADS):
                sout_ref[0, layer, d, h] = st_ref[d, h].T


def _scan(q, kf, bf, kb, bb, v, s0, prev_states, layer, *, lat_b, lat_t, ctx_b, ctx_t):
    n_tok, d = q.shape
    lat_nc, ctx_nc = lat_t // CHUNK, ctx_t // CHUNK
    lat_chunks = lat_b * lat_nc

    def seq_of(g):
        is_lat = g < lat_chunks
        cs = (g - lat_chunks) // ctx_nc
        start = jnp.where(is_lat, (g // lat_nc) * lat_nc, lat_chunks + cs * ctx_nc)
        return start, jnp.where(is_lat, lat_nc, ctx_nc), jnp.where(is_lat, 0, cs)

    def fwd(g):
        return (g, 0)

    def bwd(g):
        start, nc, _ = seq_of(g)
        return (2 * start + nc - 1 - g, 0)

    blk = lambda im: pl.BlockSpec((CHUNK, d), im)
    state = (2, HEADS, HEAD_DIM, HEAD_DIM)
    ctx_states = lambda n: pl.BlockSpec((1, n) + state, lambda g: (seq_of(g)[2], 0, 0, 0, 0, 0))
    in_specs = [blk(fwd), blk(fwd), blk(fwd), blk(fwd), blk(bwd), blk(bwd), blk(bwd), blk(bwd),
                pl.BlockSpec((1, 1) + state, lambda g: (jnp.minimum(g // lat_nc, lat_b - 1), layer, 0, 0, 0, 0))]
    args = [q, kf, bf, v, q, kb, bb, v, s0]
    if layer > 0:
        in_specs.append(ctx_states(layer))
        args.append(prev_states)
    return pl.pallas_call(
        functools.partial(_scan_kernel, layer=layer, lat_chunks=lat_chunks, lat_nc=lat_nc, ctx_nc=ctx_nc),
        grid=(n_tok // CHUNK,),
        in_specs=in_specs,
        out_specs=[blk(fwd), blk(bwd), ctx_states(layer + 1)],
        out_shape=[jax.ShapeDtypeStruct((n_tok, d), F32), jax.ShapeDtypeStruct((n_tok, d), F32),
                   jax.ShapeDtypeStruct((ctx_b, layer + 1) + state, F32)],
        scratch_shapes=[pltpu.VMEM((2, HEADS, HEAD_DIM, HEAD_DIM), F32),
                        pltpu.VMEM((2, HEADS, CHUNK, CHUNK), F32)],
        compiler_params=_params("arbitrary"),
        name="hgrn_scan",
    )(*args)


def _hgrn_out_mlp_kernel(of_ref, ob_ref, g_ref, x_ref, mod_ref, w_ref, nw_ref, w1_ref, w2_ref, o_ref):
    m = mod_ref[0]
    o = of_ref[...] + ob_ref[...]
    parts = []
    for h in range(HEADS):
        oh = o[:, h * HEAD_DIM:(h + 1) * HEAD_DIM]
        ms = jnp.mean(oh * oh, axis=-1, keepdims=True)
        parts.append(oh * lax.rsqrt(ms + EPS))
    on = jnp.concatenate(parts, axis=1) * g_ref[...].astype(F32)
    x = x_ref[...] + m[2:3] * _dot(on.astype(BF16), w_ref[...])
    o_ref[...] = _mlp_stage(x, m, nw_ref[...], w1_ref, w2_ref)


def _conv_in_kernel(x_ref, mod_ref, nw_ref, w_ref, u_ref):
    d = x_ref.shape[1]
    m = mod_ref[0]
    hn = _norm_mod(x_ref[...], nw_ref[...], m[0:1], m[1:2]).astype(BF16)
    for c in range(0, d, GLU_COLS):
        a = _dot(hn, w_ref[:, c:c + GLU_COLS])
        gate = _dot(hn, w_ref[:, d + c:d + c + GLU_COLS])
        u_ref[:, c:c + GLU_COLS] = a * _sigmoid(gate)


def _conv_out_kernel(up_ref, u_ref, un_ref, x_ref, mod_ref, wdw_ref, bdw_ref, lng_ref, lnb_ref, w_ref,
                     o_ref, ext_ref, sh_ref, wb_ref, cv_ref, *, n_lat_tiles, tiles_per_lat_seq):
    i = pl.program_id(0)
    tm, d = u_ref.shape
    pos = i % tiles_per_lat_seq
    is_lat = i < n_lat_tiles
    first = jnp.logical_or(jnp.logical_not(is_lat), pos == 0)
    last = jnp.logical_or(jnp.logical_not(is_lat), pos == tiles_per_lat_seq - 1)
    ext_ref[0:HALO, :] = jnp.where(first, 0.0, up_ref[...])
    ext_ref[HALO:HALO + tm, :] = u_ref[...]
    ext_ref[HALO + tm:HALO + tm + HALO, :] = jnp.where(last, 0.0, un_ref[...])

    n_sh = sh_ref.shape[1]
    for s in range(1, SUBLANES):
        sh_ref[s - 1] = ext_ref[s:s + n_sh, :]

    @pl.when(i == 0)
    def _():
        for t in range(CONV_K):
            wb_ref[t] = jnp.broadcast_to(wdw_ref[t:t + 1, :], (SUBLANES, d))

    groups, lanes = 8, 128
    rows = groups * SUBLANES
    off = HALO - CONV_K // 2
    n_a = -(-(CONV_K + off) // SUBLANES)

    def body(r, carry):
        base = pl.multiple_of(r * rows, rows)
        for lc in range(d // lanes):
            ls = slice(lc * lanes, (lc + 1) * lanes)
            accs = [jnp.broadcast_to(bdw_ref[:, ls], (SUBLANES, lanes))] * groups
            for s in range(SUBLANES):
                taps = [(a, a * SUBLANES + s - off) for a in range(n_a)
                        if 0 <= a * SUBLANES + s - off < CONV_K]
                span = rows + max(a for a, _ in taps) * SUBLANES
                if s == 0:
                    blk = ext_ref[pl.ds(base, span), ls]
                else:
                    blk = sh_ref[s - 1, pl.ds(base, span), ls]
                for a, t in taps:
                    w = wb_ref[t, :, ls]
                    accs = [acc + w * blk[(gi + a) * SUBLANES:(gi + a + 1) * SUBLANES]
                            for gi, acc in enumerate(accs)]
            cv_ref[pl.ds(base, rows), ls] = jnp.concatenate(accs, axis=0)
        return carry

    lax.fori_loop(0, tm // rows, body, 0)

    u = cv_ref[...]
    mu = jnp.mean(u, axis=-1, keepdims=True)
    uc = u - mu
    rstd = lax.rsqrt(jnp.mean(uc * uc, axis=-1, keepdims=True) + EPS)
    out = None
    for c in range(0, d, K_SLICE):
        cs = slice(c, c + K_SLICE)
        y = uc[:, cs] * rstd * lng_ref[:, cs] + lnb_ref[:, cs]
        part = _dot(_silu(y).astype(BF16), w_ref[cs, :])
        out = part if out is None else out + part
    o_ref[...] = x_ref[...] + mod_ref[0][2:3] * out


def _grid_pos_tables(n_tok, dim):
    nf = dim // 4
    omega = 1.0 / (10000.0 ** (jnp.arange(nf, dtype=F32) / nf))
    er = jnp.arange(n_tok // GRID_W, dtype=F32)[:, None] * omega
    ec = jnp.arange(GRID_W, dtype=F32)[:, None] * omega
    return (jnp.concatenate([jnp.sin(er), jnp.cos(er)], axis=-1),
            jnp.concatenate([jnp.sin(ec), jnp.cos(ec)], axis=-1))


def _lower_bounds(p):
    p = jax.nn.softmax(p.astype(F32), axis=0)
    return jnp.maximum(jnp.cumsum(p, axis=0) - p[0], 0.0)


def kernel(x_prompt, x_sample, c, state_hgrn, c_ctx, w_mod, b_mod, norm_mix, norm_mlp, hgrn_w_in, hgrn_lb_fwd, hgrn_lb_bwd, hgrn_g_norm, hgrn_w_out, conv_w_pw1, conv_w_dw, conv_b_dw, conv_ln_g, conv_ln_b, conv_w_pw2, mlp_w1, mlp_w2, final_norm):
    ctx_b, ctx_t, d = x_prompt.shape
    lat_b, lat_t, _ = x_sample.shape
    depth = w_mod.shape[0]
    d_ff = mlp_w1.shape[2]
    n_lat, n_ctx = lat_b * lat_t, ctx_b * ctx_t
    n_tok = n_lat + n_ctx
    assert d == HEADS * HEAD_DIM and lat_b < MOD_ROWS
    assert depth % 2 == 0
    assert lat_t % TM == 0 and n_ctx % TM == 0 and ctx_t % CHUNK == 0 and ctx_t == TM_CONV and d_ff % TF == 0
    assert TM % GRID_W == 0 and (TM // GRID_W) % SUBLANES == 0

    def mod_row(tm):
        return lambda i, *_: (jnp.where(i * tm < n_lat, (i * tm) // lat_t, lat_b), 0, 0)

    def mod_spec(tm):
        return pl.BlockSpec((1, N_MOD, d), mod_row(tm))

    def tok_spec(tm, width=d):
        return pl.BlockSpec((tm, width), lambda i, *_: (i, 0))

    def full_spec(shape):
        return pl.BlockSpec(shape, lambda i, *_: (0,) * len(shape))

    def resident_spec(shape):
        return pl.BlockSpec(shape, lambda i, *_: (0,) * len(shape), pipeline_mode=pl.Buffered(1))

    row = lambda a: a.reshape(1, -1)
    tok_f32 = jax.ShapeDtypeStruct((n_tok, d), F32)
    tok_bf16 = jax.ShapeDtypeStruct((n_tok, d), BF16)

    cvec = jnp.zeros((MOD_ROWS, d), F32).at[:lat_b].set(c).at[lat_b].set(c_ctx)
    mods = _modulation(cvec, w_mod, b_mod).reshape(depth, MOD_ROWS, N_MOD, d)

    lbs_f = _lower_bounds(hgrn_lb_fwd)
    lbs_b = _lower_bounds(hgrn_lb_bwd)

    pos_row, pos_col = _grid_pos_tables(lat_t, d)
    n_lat_tiles = n_lat // TM
    tiles_per_lat_seq = lat_t // TM
    first_specs = [pl.BlockSpec((TM, d), lambda i: (jnp.minimum(i, n_lat_tiles - 1), 0)),
                   pl.BlockSpec((TM // GRID_W, d // 2), lambda i: (i % tiles_per_lat_seq, 0)),
                   full_spec((GRID_W, d // 2)),
                   pl.BlockSpec((TM, d), lambda i: (jnp.maximum(i - n_lat_tiles, 0), 0))]
    x = None

    new_states = None
    for i in range(depth):
        mod = mods[i]
        if i % 2 == 0:
            a = i // 2
            first = i == 0
            x_args = ((x_sample.reshape(n_lat, d), pos_row, pos_col, x_prompt.reshape(n_ctx, d))
                      if first else (x,))
            outs = pl.pallas_call(
                functools.partial(_hgrn_in_kernel, first=first, n_lat_tiles=n_lat_tiles),
                grid=(n_tok // TM,),
                in_specs=(first_specs if first else [tok_spec(TM)]) + [
                    mod_spec(TM), full_spec((1, d)), resident_spec((d, 5 * d)),
                    full_spec((1, d)), full_spec((1, d)), full_spec((1, d))],
                out_specs=[tok_spec(TM)] * (8 if first else 7),
                out_shape=[tok_bf16, tok_bf16, tok_bf16, tok_f32, tok_bf16, tok_f32, tok_bf16]
                + ([tok_f32] if first else []),
                compiler_params=_params("arbitrary"),
                name="hgrn_in",
            )(*x_args, mod, row(norm_mix[i]), hgrn_w_in[a].astype(BF16), row(lbs_f[a]), row(lbs_b[a]),
              row(hgrn_g_norm[a]))
            q, v, kf, bf, kb, bb, g = outs[:7]
            if first:
                x = outs[7]
            of, ob, new_states = _scan(q, kf, bf, kb, bb, v, state_hgrn, new_states, a, lat_b=lat_b, lat_t=lat_t,
                                       ctx_b=ctx_b, ctx_t=ctx_t)
            x = pl.pallas_call(
                _hgrn_out_mlp_kernel,
                grid=(n_tok // TM,),
                in_specs=[tok_spec(TM), tok_spec(TM), tok_spec(TM), tok_spec(TM), mod_spec(TM),
                          full_spec((d, d)), full_spec((1, d)), resident_spec((d, d_ff)),
                          resident_spec((d_ff, d))],
                out_specs=tok_spec(TM),
                out_shape=tok_f32,
                compiler_params=_params("parallel"),
                name="hgrn_out_mlp",
            )(of, ob, g, x, mod, hgrn_w_out[a].astype(BF16), row(norm_mlp[i]), mlp_w1[i].astype(BF16),
              mlp_w2[i].astype(BF16))
            continue

        b = i // 2
        u = pl.pallas_call(
            _conv_in_kernel,
            grid=(n_tok // TM,),
            in_specs=[tok_spec(TM), mod_spec(TM), full_spec((1, d)), full_spec((d, 2 * d))],
            out_specs=tok_spec(TM),
            out_shape=tok_f32,
            compiler_params=_params("parallel"),
            name="conv_in",
        )(x, mod, row(norm_mix[i]), conv_w_pw1[b].astype(BF16))
        tm = TM_CONV
        hb = tm // HALO
        n_halo = n_tok // HALO
        wdw = jnp.zeros((CONV_K + 1, d), F32).at[:CONV_K].set(conv_w_dw[b])
        x = pl.pallas_call(
            functools.partial(_conv_out_kernel, n_lat_tiles=n_lat // tm, tiles_per_lat_seq=lat_t // tm),
            grid=(n_tok // tm,),
            in_specs=[pl.BlockSpec((HALO, d), lambda i: (jnp.maximum(i * hb - 1, 0), 0)),
                      tok_spec(tm),
                      pl.BlockSpec((HALO, d), lambda i: (jnp.minimum((i + 1) * hb, n_halo - 1), 0)),
                      tok_spec(tm), mod_spec(tm), full_spec((CONV_K + 1, d)), full_spec((1, d)),
                      full_spec((1, d)), full_spec((1, d)), full_spec((d, d))],
            out_specs=tok_spec(tm),
            out_shape=tok_f32,
            scratch_shapes=[pltpu.VMEM((tm + 2 * HALO, d), F32),
                            pltpu.VMEM((SUBLANES - 1, tm + 2 * HALO - SUBLANES, d), F32),
                            pltpu.VMEM((CONV_K, SUBLANES, d), F32),
                            pltpu.VMEM((tm, d), F32)],
            compiler_params=_params("arbitrary"),
            name="conv_out",
        )(u, u, u, x, mod, wdw, row(conv_b_dw[b]), row(conv_ln_g[b]), row(conv_ln_b[b]),
          conv_w_pw2[b].astype(BF16))

        final = i == depth - 1
        if final:
            out_specs = [pl.BlockSpec((TM, d), lambda i: (jnp.minimum(i, n_lat_tiles - 1), 0)),
                         pl.BlockSpec((TM, d), lambda i: (jnp.maximum(i - n_lat_tiles, 0), 0))]
            out_shape = [jax.ShapeDtypeStruct((n_lat, d), F32), jax.ShapeDtypeStruct((n_ctx, d), F32)]
        else:
            out_specs, out_shape = tok_spec(TM), tok_f32
        x = pl.pallas_call(
            functools.partial(_mlp_kernel, final=final, n_lat_tiles=n_lat_tiles),
            grid=(n_tok // TM,),
            in_specs=[tok_spec(TM), mod_spec(TM), full_spec((1, d)), full_spec((1, d)),
                      resident_spec((d, d_ff)), resident_spec((d_ff, d))],
            out_specs=out_specs,
            out_shape=out_shape,
            compiler_params=_params("arbitrary"),
            name="mlp",
        )(x, mod, row(norm_mlp[i]), row(final_norm), mlp_w1[i].astype(BF16), mlp_w2[i].astype(BF16))

    y_lat, y_ctx = x
    y_sample = y_lat.reshape(lat_b, lat_t, d)
    y_prompt = y_ctx.reshape(ctx_b, ctx_t, d)
    return (y_prompt, y_sample, new_states.astype(x_prompt.dtype))
```

```python
import functools

import jax
import jax.numpy as jnp
from jax import lax
from jax.experimental import pallas as pl
from jax.experimental.pallas import tpu as pltpu

F32 = jnp.float32
BF16 = jnp.bfloat16

N_MOD = 6
HEADS = 8
HEAD_DIM = 128
GRID_W = 64
CONV_K = 31
EPS = 1e-6
K_MAX = 1.0 - 1e-6

CHUNK = 128
SUB = 16
SUB_WIDE = 32
N_SUB = CHUNK // SUB
FAST_EXP_MAX = 118.0
SUBLANES = 8
HALO = 16
MOD_ROWS = 8

TM = 512
TM_CONV = 256
TF = 1024
GLU_COLS = 256
K_SLICE = 256
VMEM_LIMIT = 56 * 1024 * 1024


def _params(*sem):
    return pltpu.CompilerParams(dimension_semantics=sem, vmem_limit_bytes=VMEM_LIMIT)


def _dot(a, b):
    return jnp.dot(a, b, preferred_element_type=F32)


def _dot_nt(a, b):
    return lax.dot_general(a, b, (((1,), (1,)), ((), ())), preferred_element_type=F32)


def _sigmoid(x):
    return 1.0 / (1.0 + jnp.exp(-x))


def _silu(x):
    return x * _sigmoid(x)


def _norm_mod(x, nw, shift, scale):
    ms = jnp.mean(x * x, axis=-1, keepdims=True)
    y = x * lax.rsqrt(ms + EPS) * nw
    return y * (1.0 + scale) + shift


def _mod_kernel(c_ref, w_ref, b_ref, o_ref):
    c = _silu(c_ref[...]).astype(BF16)
    o_ref[0] = _dot(c, w_ref[0].astype(BF16)) + b_ref[0]


def _modulation(cvec, w_mod, b_mod):
    depth, d, n = w_mod.shape
    tn = 1536
    return pl.pallas_call(
        _mod_kernel,
        grid=(depth, n // tn),
        in_specs=[
            pl.BlockSpec((MOD_ROWS, d), lambda l, j: (0, 0)),
            pl.BlockSpec((1, d, tn), lambda l, j: (l, 0, j)),
            pl.BlockSpec((1, 1, tn), lambda l, j: (l, 0, j)),
        ],
        out_specs=pl.BlockSpec((1, MOD_ROWS, tn), lambda l, j: (l, 0, j)),
        out_shape=jax.ShapeDtypeStruct((depth, MOD_ROWS, n), F32),
        compiler_params=_params("parallel", "parallel"),
        name="modulation",
    )(cvec, w_mod, b_mod.reshape(depth, 1, n))


def _mlp_stage(x, m, nw, w1_ref, w2_ref):
    hn = _norm_mod(x, nw, m[3:4], m[4:5]).astype(BF16)
    acc = None
    for j in range(w1_ref.shape[1] // TF):
        h = _dot(hn, w1_ref[:, j * TF:(j + 1) * TF])
        h = jnp.square(jnp.maximum(h, 0.0)).astype(BF16)
        part = _dot(h, w2_ref[j * TF:(j + 1) * TF, :])
        acc = part if acc is None else acc + part
    return x + m[5:6] * acc


def _mlp_kernel(x_ref, mod_ref, nw_ref, fw_ref, w1_ref, w2_ref, *o_refs, final, n_lat_tiles):
    y = _mlp_stage(x_ref[...], mod_ref[0], nw_ref[...], w1_ref, w2_ref)
    if not final:
        o_refs[0][...] = y
        return
    ms = jnp.mean(y * y, axis=-1, keepdims=True)
    y = y * lax.rsqrt(ms + EPS) * fw_ref[...]
    lat_ref, ctx_ref = o_refs
    is_lat = pl.program_id(0) < n_lat_tiles

    @pl.when(is_lat)
    def _():
        lat_ref[...] = y

    @pl.when(jnp.logical_not(is_lat))
    def _():
        ctx_ref[...] = y


def _hgrn_in_kernel(*refs, first, n_lat_tiles):
    if first:
        (xs_ref, prow_ref, pcol_ref, xp_ref, mod_ref, nw_ref, w_ref, lbf_ref, lbb_ref, gn_ref,
         q_ref, v_ref, kf_ref, bf_ref, kb_ref, bb_ref, g_ref, x0_ref) = refs
        is_lat = pl.program_id(0) < n_lat_tiles
        prow, pcol = prow_ref[...], pcol_ref[...]
        n_rows, half = prow.shape
        left = jnp.concatenate([jnp.broadcast_to(prow[r:r + 1], (GRID_W, half)) for r in range(n_rows)], axis=0)
        pos = jnp.concatenate([left, jnp.concatenate([pcol] * n_rows, axis=0)], axis=1)
        x = jnp.where(is_lat, xs_ref[...] + pos, xp_ref[...])
        x0_ref[...] = x
    else:
        (x_ref, mod_ref, nw_ref, w_ref, lbf_ref, lbb_ref, gn_ref,
         q_ref, v_ref, kf_ref, bf_ref, kb_ref, bb_ref, g_ref) = refs
        x = x_ref[...]
    tm, d = x.shape
    m = mod_ref[0]
    hn = _norm_mod(x, nw_ref[...], m[0:1], m[1:2]).astype(BF16)

    def proj(j):
        return _dot(hn, w_ref[:, j * d:(j + 1) * d])

    def gate(j, lb_ref, k_ref):
        k = jnp.minimum((1.0 - lb_ref[...]) * _sigmoid(-proj(j)), K_MAX)
        k_ref[...] = k.astype(BF16)
        lg = jnp.log2(1.0 - k)
        terms = []
        for r in range(tm // CHUNK):
            x = lg[r * CHUNK:(r + 1) * CHUNK]
            hi = x.astype(BF16)
            terms.append(jnp.concatenate([hi, (x - hi.astype(F32)).astype(BF16)], axis=0))
        return terms

    def cumulate(terms, b_ref, reverse):
        row = lax.broadcasted_iota(jnp.int32, (CHUNK, CHUNK), 0)
        col = lax.broadcasted_iota(jnp.int32, (CHUNK, CHUNK), 1)
        tri = jnp.where((col >= row) if reverse else (col <= row), 1.0, 0.0).astype(BF16)
        tri2 = jnp.concatenate([tri, tri], axis=1)
        for r, term in enumerate(terms):
            b_ref[r * CHUNK:(r + 1) * CHUNK, :] = _dot(tri2, term)

    terms_f = gate(2, lbf_ref, kf_ref)
    terms_b = gate(3, lbb_ref, kb_ref)
    cumulate(terms_f, bf_ref, False)
    q_ref[...] = (_silu(proj(0)) * (HEAD_DIM ** -0.5)).astype(BF16)
    cumulate(terms_b, bb_ref, True)
    g_ref[...] = (gn_ref[...] * _silu(proj(4))).astype(BF16)
    v_ref[...] = proj(1).astype(BF16)


def _block_edges(b, reverse, sub):
    n = CHUNK // sub
    edge = 0 if reverse else sub - 1
    far = jnp.concatenate([b[i * sub + edge:i * sub + edge + 1] for i in range(n)], axis=0)
    zero = jnp.zeros_like(far[:1])
    ref = jnp.concatenate([far[1:], zero] if reverse else [zero, far[:-1]], axis=0)
    return far, ref


def _key_order(x, reverse, sub):
    if not reverse:
        return x
    return jnp.concatenate([x[i * sub:(i + 1) * sub] for i in reversed(range(CHUNK // sub))], axis=0)


def _scan_scores(q, k, b, reverse, sub):
    n = CHUNK // sub
    q = q.astype(F32)
    k = k.astype(F32)
    row = lax.broadcasted_iota(jnp.int32, (CHUNK, CHUNK), 0)
    col = lax.broadcasted_iota(jnp.int32, (CHUNK, CHUNK), 1)
    if reverse:
        key = (n - 1 - col // sub) * sub + col % sub
        keep = key >= row
    else:
        keep = col <= row
    far, ref = _block_edges(b, reverse, sub)
    qs, ks = [], []
    for i in range(n):
        rows = slice(i * sub, (i + 1) * sub)
        qs.append((q[rows] * jnp.exp2(b[rows] - ref[i:i + 1])).astype(BF16))
        ks.append(k[rows] * jnp.exp2(far[i:i + 1] - b[rows]))
    blocks = []
    for i in range(n):
        scale = jnp.exp2(ref[i:i + 1] - far)
        sources = range(n - 1, i - 1, -1) if reverse else range(i + 1)
        keys = jnp.concatenate([(ks[j] * scale[j:j + 1]).astype(BF16) for j in sources], axis=0)
        a = _dot_nt(qs[i], keys)
        if len(sources) < n:
            a = jnp.concatenate([a, jnp.zeros((sub, CHUNK - len(sources) * sub), F32)], axis=1)
        blocks.append(a)
    return jnp.where(keep, jnp.concatenate(blocks, axis=0), 0.0)


def _max_block_exponent(b_ref, reverse, sub):
    n = CHUNK // sub
    edge = 0 if reverse else sub - 1
    far = [b_ref[i * sub + edge:i * sub + edge + 1, :] for i in range(n)]
    if reverse:
        worst = -far[n - 1]
        for i in range(n - 1):
            worst = jnp.maximum(worst, far[i + 1] - far[i])
    else:
        worst = -far[0]
        for i in range(1, n):
            worst = jnp.maximum(worst, far[i - 1] - far[i])
    return jnp.max(worst)


def _exact_diag_blocks(q_ref, k_ref, b_ref, a_ref, reverse):
    lane = lax.broadcasted_iota(jnp.int32, (SUB, HEAD_DIM), 1)
    rowi = lax.broadcasted_iota(jnp.int32, (SUB, HEAD_DIM), 0)

    def block(idx, carry):
        h = idx // N_SUB
        i = idx % N_SUB
        r0 = pl.multiple_of(i * SUB, SUB)
        c0 = (N_SUB - 1 - i) * SUB if reverse else r0
        lanes = pl.ds(pl.multiple_of(h * HEAD_DIM, HEAD_DIM), HEAD_DIM)
        qi = q_ref[pl.ds(r0, SUB), lanes].astype(F32)
        ki = k_ref[pl.ds(r0, SUB), lanes].astype(F32)
        bi = b_ref[pl.ds(r0, SUB), lanes]

        def column(s, acc):
            pick = rowi == s
            ks = jnp.sum(jnp.where(pick, ki, 0.0), axis=0, keepdims=True)
            bs = jnp.sum(jnp.where(pick, bi, 0.0), axis=0, keepdims=True)
            w = jnp.exp2(jnp.minimum(bi - bs, 0.0))
            c = jnp.sum(qi * ks * w, axis=-1, keepdims=True)
            return jnp.where(lane == c0 + s, c, acc)

        acc = lax.fori_loop(0, SUB, column, jnp.zeros((SUB, HEAD_DIM), F32))
        keep = (lane - c0 >= rowi) if reverse else (lane - c0 <= rowi)
        in_block = jnp.logical_and(lane >= c0, lane < c0 + SUB)
        old = a_ref[h, pl.ds(r0, SUB), :]
        a_ref[h, pl.ds(r0, SUB), :] = jnp.where(in_block, jnp.where(keep, acc, 0.0), old)
        return carry

    lax.fori_loop(0, HEADS * N_SUB, block, 0)


def _scan_apply(q, k, b, v, a, st_ref, reverse, sub):
    b_last = b[0:1] if reverse else b[CHUNK - 1:CHUNK]
    st = st_ref[...]
    vt = _key_order(v, reverse, sub).T
    lhs = jnp.concatenate([(q.astype(F32) * jnp.exp2(b)).astype(BF16), a.astype(BF16)], axis=1)
    rhs = jnp.concatenate([st.astype(BF16), vt], axis=1)
    k_end = _key_order((k.astype(F32) * jnp.exp2(b_last - b)).astype(BF16), reverse, sub)
    st_ref[...] = st * jnp.exp2(b_last) + _dot(vt, k_end)
    return _dot_nt(lhs, rhs)


def _scan_kernel(*refs, layer, lat_chunks, lat_nc, ctx_nc):
    qf_ref, kf_ref, bf_ref, vf_ref, qb_ref, kb_ref, bb_ref, vb_ref, s0_ref = refs[:9]
    prev_ref = refs[9] if layer > 0 else None
    of_ref, ob_ref, sout_ref, st_ref, a_ref = refs[-5:]
    g = pl.program_id(0)
    is_lat = g < lat_chunks
    c = jnp.where(is_lat, g % lat_nc, (g - lat_chunks) % ctx_nc)
    last = c == jnp.where(is_lat, lat_nc, ctx_nc) - 1

    @pl.when(jnp.logical_and(c == 0, is_lat))
    def _():
        for d in range(2):
            for h in range(HEADS):
                st_ref[d, h] = s0_ref[0, 0, d, h].T

    @pl.when(jnp.logical_and(c == 0, jnp.logical_not(is_lat)))
    def _():
        st_ref[...] = jnp.zeros_like(st_ref)

    dirs = ((qf_ref, kf_ref, bf_ref, vf_ref, of_ref, False), (qb_ref, kb_ref, bb_ref, vb_ref, ob_ref, True))
    head = lambda h: slice(h * HEAD_DIM, (h + 1) * HEAD_DIM)

    def worst_exponent(sub):
        return jnp.maximum(_max_block_exponent(bf_ref, False, sub), _max_block_exponent(bb_ref, True, sub))

    def chunk(sub):
        for h in range(HEADS):
            for d, (q_ref, k_ref, b_ref, _, _, reverse) in enumerate(dirs):
                a_ref[d, h] = _scan_scores(q_ref[:, head(h)], k_ref[:, head(h)], b_ref[:, head(h)], reverse, sub)

        if sub == SUB:
            @pl.when(worst_exponent(SUB) > FAST_EXP_MAX)
            def _():
                for d, (q_ref, k_ref, b_ref, _, _, reverse) in enumerate(dirs):
                    _exact_diag_blocks(q_ref, k_ref, b_ref, a_ref.at[d], reverse)

        for h in range(HEADS):
            for d, (q_ref, k_ref, b_ref, v_ref, o_ref, reverse) in enumerate(dirs):
                o_ref[:, head(h)] = _scan_apply(q_ref[:, head(h)], k_ref[:, head(h)], b_ref[:, head(h)],
                                                v_ref[:, head(h)], a_ref[d, h], st_ref.at[d, h], reverse, sub)

    wide_ok = worst_exponent(SUB_WIDE) <= FAST_EXP_MAX

    @pl.when(wide_ok)
    def _():
        chunk(SUB_WIDE)

    @pl.when(jnp.logical_not(wide_ok))
    def _():
        chunk(SUB)

    @pl.when(jnp.logical_and(last, jnp.logical_not(is_lat)))
    def _():
        for l in range(layer):
            sout_ref[0, l] = prev_ref[0, l]
        for d in range(2):
            for h in range(HEADS):
                sout_ref[0, layer, d, h] = st_ref[d, h].T


def _scan(q, kf, bf, kb, bb, v, s0, prev_states, layer, *, lat_b, lat_t, ctx_b, ctx_t):
    n_tok, d = q.shape
    lat_nc, ctx_nc = lat_t // CHUNK, ctx_t // CHUNK
    lat_chunks = lat_b * lat_nc

    def seq_of(g):
        is_lat = g < lat_chunks
        cs = (g - lat_chunks) // ctx_nc
        start = jnp.where(is_lat, (g // lat_nc) * lat_nc, lat_chunks + cs * ctx_nc)
        return start, jnp.where(is_lat, lat_nc, ctx_nc), jnp.where(is_lat, 0, cs)

    def fwd(g):
        return (g, 0)

    def bwd(g):
        start, nc, _ = seq_of(g)
        return (2 * start + nc - 1 - g, 0)

    blk = lambda im: pl.BlockSpec((CHUNK, d), im)
    state = (2, HEADS, HEAD_DIM, HEAD_DIM)
    ctx_states = lambda n: pl.BlockSpec((1, n) + state, lambda g: (seq_of(g)[2], 0, 0, 0, 0, 0))
    in_specs = [blk(fwd), blk(fwd), blk(fwd), blk(fwd), blk(bwd), blk(bwd), blk(bwd), blk(bwd),
                pl.BlockSpec((1, 1) + state, lambda g: (jnp.minimum(g // lat_nc, lat_b - 1), layer, 0, 0, 0, 0))]
    args = [q, kf, bf, v, q, kb, bb, v, s0]
    if layer > 0:
        in_specs.append(ctx_states(layer))
        args.append(prev_states)
    return pl.pallas_call(
        functools.partial(_scan_kernel, layer=layer, lat_chunks=lat_chunks, lat_nc=lat_nc, ctx_nc=ctx_nc),
        grid=(n_tok // CHUNK,),
        in_specs=in_specs,
        out_specs=[blk(fwd), blk(bwd), ctx_states(layer + 1)],
        out_shape=[jax.ShapeDtypeStruct((n_tok, d), F32), jax.ShapeDtypeStruct((n_tok, d), F32),
                   jax.ShapeDtypeStruct((ctx_b, layer + 1) + state, F32)],
        scratch_shapes=[pltpu.VMEM((2, HEADS, HEAD_DIM, HEAD_DIM), F32),
                        pltpu.VMEM((2, HEADS, CHUNK, CHUNK), F32)],
        compiler_params=_params("arbitrary"),
        name="hgrn_scan",
    )(*args)


def _hgrn_out_mlp_kernel(of_ref, ob_ref, g_ref, x_ref, mod_ref, w_ref, nw_ref, w1_ref, w2_ref, o_ref):
    m = mod_ref[0]
    o = of_ref[...] + ob_ref[...]
    parts = []
    for h in range(HEADS):
        oh = o[:, h * HEAD_DIM:(h + 1) * HEAD_DIM]
        ms = jnp.mean(oh * oh, axis=-1, keepdims=True)
        parts.append(oh * lax.rsqrt(ms + EPS))
    on = jnp.concatenate(parts, axis=1) * g_ref[...].astype(F32)
    x = x_ref[...] + m[2:3] * _dot(on.astype(BF16), w_ref[...])
    o_ref[...] = _mlp_stage(x, m, nw_ref[...], w1_ref, w2_ref)


def _conv_in_kernel(x_ref, mod_ref, nw_ref, w_ref, u_ref):
    d = x_ref.shape[1]
    m = mod_ref[0]
    hn = _norm_mod(x_ref[...], nw_ref[...], m[0:1], m[1:2]).astype(BF16)
    for c in range(0, d, GLU_COLS):
        a = _dot(hn, w_ref[:, c:c + GLU_COLS])
        gate = _dot(hn, w_ref[:, d + c:d + c + GLU_COLS])
        u_ref[:, c:c + GLU_COLS] = a * _sigmoid(gate)


def _conv_out_kernel(up_ref, u_ref, un_ref, x_ref, mod_ref, wdw_ref, bdw_ref, lng_ref, lnb_ref, w_ref,
                     o_ref, ext_ref, sh_ref, wb_ref, cv_ref, *, n_lat_tiles, tiles_per_lat_seq):
    i = pl.program_id(0)
    tm, d = u_ref.shape
    pos = i % tiles_per_lat_seq
    is_lat = i < n_lat_tiles
    first = jnp.logical_or(jnp.logical_not(is_lat), pos == 0)
    last = jnp.logical_or(jnp.logical_not(is_lat), pos == tiles_per_lat_seq - 1)
    ext_ref[0:HALO, :] = jnp.where(first, 0.0, up_ref[...])
    ext_ref[HALO:HALO + tm, :] = u_ref[...]
    ext_ref[HALO + tm:HALO + tm + HALO, :] = jnp.where(last, 0.0, un_ref[...])

    n_sh = sh_ref.shape[1]
    for s in range(1, SUBLANES):
        sh_ref[s - 1] = ext_ref[s:s + n_sh, :]

    @pl.when(i == 0)
    def _():
        for t in range(CONV_K):
            wb_ref[t] = jnp.broadcast_to(wdw_ref[t:t + 1, :], (SUBLANES, d))

    groups, lanes = 8, 128
    rows = groups * SUBLANES
    off = HALO - CONV_K // 2
    n_a = -(-(CONV_K + off) // SUBLANES)

    def body(r, carry):
        base = pl.multiple_of(r * rows, rows)
        for lc in range(d // lanes):
            ls = slice(lc * lanes, (lc + 1) * lanes)
            accs = [jnp.broadcast_to(bdw_ref[:, ls], (SUBLANES, lanes))] * groups
            for s in range(SUBLANES):
                taps = [(a, a * SUBLANES + s - off) for a in range(n_a)
                        if 0 <= a * SUBLANES + s - off < CONV_K]
                span = rows + max(a for a, _ in taps) * SUBLANES
                if s == 0:
                    blk = ext_ref[pl.ds(base, span), ls]
                else:
                    blk = sh_ref[s - 1, pl.ds(base, span), ls]
                for a, t in taps:
                    w = wb_ref[t, :, ls]
                    accs = [acc + w * blk[(gi + a) * SUBLANES:(gi + a + 1) * SUBLANES]
                            for gi, acc in enumerate(accs)]
            cv_ref[pl.ds(base, rows), ls] = jnp.concatenate(accs, axis=0)
        return carry

    lax.fori_loop(0, tm // rows, body, 0)

    u = cv_ref[...]
    mu = jnp.mean(u, axis=-1, keepdims=True)
    uc = u - mu
    rstd = lax.rsqrt(jnp.mean(uc * uc, axis=-1, keepdims=True) + EPS)
    out = None
    for c in range(0, d, K_SLICE):
        cs = slice(c, c + K_SLICE)
        y = uc[:, cs] * rstd * lng_ref[:, cs] + lnb_ref[:, cs]
        part = _dot(_silu(y).astype(BF16), w_ref[cs, :])
        out = part if out is None else out + part
    o_ref[...] = x_ref[...] + mod_ref[0][2:3] * out


def _grid_pos_tables(n_tok, dim):
    nf = dim // 4
    omega = 1.0 / (10000.0 ** (jnp.arange(nf, dtype=F32) / nf))
    er = jnp.arange(n_tok // GRID_W, dtype=F32)[:, None] * omega
    ec = jnp.arange(GRID_W, dtype=F32)[:, None] * omega
    return (jnp.concatenate([jnp.sin(er), jnp.cos(er)], axis=-1),
            jnp.concatenate([jnp.sin(ec), jnp.cos(ec)], axis=-1))


def _lower_bounds(p):
    p = jax.nn.softmax(p.astype(F32), axis=0)
    return jnp.maximum(jnp.cumsum(p, axis=0) - p[0], 0.0)


def kernel(x_prompt, x_sample, c, state_hgrn, c_ctx, w_mod, b_mod, norm_mix, norm_mlp, hgrn_w_in, hgrn_lb_fwd, hgrn_lb_bwd, hgrn_g_norm, hgrn_w_out, conv_w_pw1, conv_w_dw, conv_b_dw, conv_ln_g, conv_ln_b, conv_w_pw2, mlp_w1, mlp_w2, final_norm):
    ctx_b, ctx_t, d = x_prompt.shape
    lat_b, lat_t, _ = x_sample.shape
    depth = w_mod.shape[0]
    d_ff = mlp_w1.shape[2]
    n_lat, n_ctx = lat_b * lat_t, ctx_b * ctx_t
    n_tok = n_lat + n_ctx
    assert d == HEADS * HEAD_DIM and lat_b < MOD_ROWS
    assert depth % 2 == 0
    assert lat_t % TM == 0 and n_ctx % TM == 0 and ctx_t % CHUNK == 0 and ctx_t == TM_CONV and d_ff % TF == 0
    assert TM % GRID_W == 0 and (TM // GRID_W) % SUBLANES == 0

    def mod_row(tm):
        return lambda i, *_: (jnp.where(i * tm < n_lat, (i * tm) // lat_t, lat_b), 0, 0)

    def mod_spec(tm):
        return pl.BlockSpec((1, N_MOD, d), mod_row(tm))

    def tok_spec(tm, width=d):
        return pl.BlockSpec((tm, width), lambda i, *_: (i, 0))

    def full_spec(shape):
        return pl.BlockSpec(shape, lambda i, *_: (0,) * len(shape))

    def resident_spec(shape):
        return pl.BlockSpec(shape, lambda i, *_: (0,) * len(shape), pipeline_mode=pl.Buffered(1))

    row = lambda a: a.reshape(1, -1)
    tok_f32 = jax.ShapeDtypeStruct((n_tok, d), F32)
    tok_bf16 = jax.ShapeDtypeStruct((n_tok, d), BF16)

    cvec = jnp.zeros((MOD_ROWS, d), F32).at[:lat_b].set(c).at[lat_b].set(c_ctx)
    mods = _modulation(cvec, w_mod, b_mod).reshape(depth, MOD_ROWS, N_MOD, d)

    lbs_f = _lower_bounds(hgrn_lb_fwd)
    lbs_b = _lower_bounds(hgrn_lb_bwd)

    pos_row, pos_col = _grid_pos_tables(lat_t, d)
    n_lat_tiles = n_lat // TM
    tiles_per_lat_seq = lat_t // TM
    first_specs = [pl.BlockSpec((TM, d), lambda i: (jnp.minimum(i, n_lat_tiles - 1), 0)),
                   pl.BlockSpec((TM // GRID_W, d // 2), lambda i: (i % tiles_per_lat_seq, 0)),
                   full_spec((GRID_W, d // 2)),
                   pl.BlockSpec((TM, d), lambda i: (jnp.maximum(i - n_lat_tiles, 0), 0))]
    x = None

    new_states = None
    for i in range(depth):
        mod = mods[i]
        if i % 2 == 0:
            a = i // 2
            first = i == 0
            x_args = ((x_sample.reshape(n_lat, d), pos_row, pos_col, x_prompt.reshape(n_ctx, d))
                      if first else (x,))
            outs = pl.pallas_call(
                functools.partial(_hgrn_in_kernel, first=first, n_lat_tiles=n_lat_tiles),
                grid=(n_tok // TM,),
                in_specs=(first_specs if first else [tok_spec(TM)]) + [
                    mod_spec(TM), full_spec((1, d)), resident_spec((d, 5 * d)),
                    full_spec((1, d)), full_spec((1, d)), full_spec((1, d))],
                out_specs=[tok_spec(TM)] * (8 if first else 7),
                out_shape=[tok_bf16, tok_bf16, tok_bf16, tok_f32, tok_bf16, tok_f32, tok_bf16]
                + ([tok_f32] if first else []),
                compiler_params=_params("arbitrary"),
                name="hgrn_in",
            )(*x_args, mod, row(norm_mix[i]), hgrn_w_in[a].astype(BF16), row(lbs_f[a]), row(lbs_b[a]),
              row(hgrn_g_norm[a]))
            q, v, kf, bf, kb, bb, g = outs[:7]
            if first:
                x = outs[7]
            of, ob, new_states = _scan(q, kf, bf, kb, bb, v, state_hgrn, new_states, a, lat_b=lat_b, lat_t=lat_t,
                                       ctx_b=ctx_b, ctx_t=ctx_t)
            x = pl.pallas_call(
                _hgrn_out_mlp_kernel,
                grid=(n_tok // TM,),
                in_specs=[tok_spec(TM), tok_spec(TM), tok_spec(TM), tok_spec(TM), mod_spec(TM),
                          full_spec((d, d)), full_spec((1, d)), resident_spec((d, d_ff)),
                          resident_spec((d_ff, d))],
                out_specs=tok_spec(TM),
                out_shape=tok_f32,
                compiler_params=_params("parallel"),
                name="hgrn_out_mlp",
            )(of, ob, g, x, mod, hgrn_w_out[a].astype(BF16), row(norm_mlp[i]), mlp_w1[i].astype(BF16),
              mlp_w2[i].astype(BF16))
            continue

        b = i // 2
        u = pl.pallas_call(
            _conv_in_kernel,
            grid=(n_tok // TM,),
            in_specs=[tok_spec(TM), mod_spec(TM), full_spec((1, d)), full_spec((d, 2 * d))],
            out_specs=tok_spec(TM),
            out_shape=tok_f32,
            compiler_params=_params("parallel"),
            name="conv_in",
        )(x, mod, row(norm_mix[i]), conv_w_pw1[b].astype(BF16))
        tm = TM_CONV
        hb = tm // HALO
        n_halo = n_tok // HALO
        wdw = jnp.zeros((CONV_K + 1, d), F32).at[:CONV_K].set(conv_w_dw[b])
        x = pl.pallas_call(
            functools.partial(_conv_out_kernel, n_lat_tiles=n_lat // tm, tiles_per_lat_seq=lat_t // tm),
            grid=(n_tok // tm,),
            in_specs=[pl.BlockSpec((HALO, d), lambda i: (jnp.maximum(i * hb - 1, 0), 0)),
                      tok_spec(tm),
                      pl.BlockSpec((HALO, d), lambda i: (jnp.minimum((i + 1) * hb, n_halo - 1), 0)),
                      tok_spec(tm), mod_spec(tm), full_spec((CONV_K + 1, d)), full_spec((1, d)),
                      full_spec((1, d)), full_spec((1, d)), full_spec((d, d))],
            out_specs=tok_spec(tm),
            out_shape=tok_f32,
            scratch_shapes=[pltpu.VMEM((tm + 2 * HALO, d), F32),
                            pltpu.VMEM((SUBLANES - 1, tm + 2 * HALO - SUBLANES, d), F32),
                            pltpu.VMEM((CONV_K, SUBLANES, d), F32),
                            pltpu.VMEM((tm, d), F32)],
            compiler_params=_params("arbitrary"),
            name="conv_out",
        )(u, u, u, x, mod, wdw, row(conv_b_dw[b]), row(conv_ln_g[b]), row(conv_ln_b[b]),
          conv_w_pw2[b].astype(BF16))

        final = i == depth - 1
        if final:
            out_specs = [pl.BlockSpec((TM, d), lambda i: (jnp.minimum(i, n_lat_tiles - 1), 0)),
                         pl.BlockSpec((TM, d), lambda i: (jnp.maximum(i - n_lat_tiles, 0), 0))]
            out_shape = [jax.ShapeDtypeStruct((n_lat, d), F32), jax.ShapeDtypeStruct((n_ctx, d), F32)]
        else:
            out_specs, out_shape = tok_spec(TM), tok_f32
        x = pl.pallas_call(
            functools.partial(_mlp_kernel, final=final, n_lat_tiles=n_lat_tiles),
            grid=(n_tok // TM,),
            in_specs=[tok_spec(TM), mod_spec(TM), full_spec((1, d)), full_spec((1, d)),
                      resident_spec((d, d_ff)), resident_spec((d_ff, d))],
            out_specs=out_specs,
            out_shape=out_shape,
            compiler_params=_params("arbitrary"),
            name="mlp",
        )(x, mod, row(norm_mlp[i]), row(final_norm), mlp_w1[i].astype(BF16), mlp_w2[i].astype(BF16))

    y_lat, y_ctx = x
    y_sample = y_lat.reshape(lat_b, lat_t, d)
    y_prompt = y_ctx.reshape(ctx_b, ctx_t, d)
    return (y_prompt, y_sample, new_states.astype(x_prompt.dtype))
```

```python
import functools

import jax
import jax.numpy as jnp
from jax import lax
from jax.experimental import pallas as pl
from jax.experimental.pallas import tpu as pltpu

F32 = jnp.float32
BF16 = jnp.bfloat16

N_MOD = 6
HEADS = 8
HEAD_DIM = 128
GRID_W = 64
CONV_K = 31
EPS = 1e-6
K_MAX = 1.0 - 1e-6

CHUNK = 128
SUB = 16
SUB_WIDE = 32
N_SUB = CHUNK // SUB
FAST_EXP_MAX = 118.0
SUBLANES = 8
HALO = 16
MOD_ROWS = 8

TM = 512
TM_CONV = 256
TF = 1024
COL_SLICE = 256
K_SLICE = 256
VMEM_LIMIT = 56 * 1024 * 1024


def _params(*sem):
    return pltpu.CompilerParams(dimension_semantics=sem, vmem_limit_bytes=VMEM_LIMIT)


def _dot(a, b):
    return jnp.dot(a, b, preferred_element_type=F32)


def _dot_nt(a, b):
    return lax.dot_general(a, b, (((1,), (1,)), ((), ())), preferred_element_type=F32)


def _sigmoid(x):
    return 1.0 / (1.0 + jnp.exp(-x))


def _silu(x):
    return x * _sigmoid(x)


def _norm_mod(x, nw, shift, scale):
    ms = jnp.mean(x * x, axis=-1, keepdims=True)
    y = x * lax.rsqrt(ms + EPS) * nw
    return y * (1.0 + scale) + shift


def _mod_kernel(c_ref, w_ref, b_ref, o_ref):
    c = _silu(c_ref[...]).astype(BF16)
    o_ref[0] = _dot(c, w_ref[0].astype(BF16)) + b_ref[0]


def _modulation(cvec, w_mod, b_mod):
    depth, d, n = w_mod.shape
    tn = 1536
    return pl.pallas_call(
        _mod_kernel,
        grid=(depth, n // tn),
        in_specs=[
            pl.BlockSpec((MOD_ROWS, d), lambda l, j: (0, 0)),
            pl.BlockSpec((1, d, tn), lambda l, j: (l, 0, j)),
            pl.BlockSpec((1, 1, tn), lambda l, j: (l, 0, j)),
        ],
        out_specs=pl.BlockSpec((1, MOD_ROWS, tn), lambda l, j: (l, 0, j)),
        out_shape=jax.ShapeDtypeStruct((depth, MOD_ROWS, n), F32),
        compiler_params=_params("parallel", "parallel"),
        name="modulation",
    )(cvec, w_mod, b_mod.reshape(depth, 1, n))


def _mlp_stage(x, m, nw, w1_ref, w2_ref):
    hn = _norm_mod(x, nw, m[3:4], m[4:5]).astype(BF16)
    acc = None
    for j in range(w1_ref.shape[1] // TF):
        h = _dot(hn, w1_ref[:, j * TF:(j + 1) * TF])
        h = jnp.square(jnp.maximum(h, 0.0)).astype(BF16)
        part = _dot(h, w2_ref[j * TF:(j + 1) * TF, :])
        acc = part if acc is None else acc + part
    return x + m[5:6] * acc


def _mlp_kernel(x_ref, mod_ref, nw_ref, fw_ref, w1_ref, w2_ref, *o_refs, final, n_lat_tiles):
    y = _mlp_stage(x_ref[...], mod_ref[0], nw_ref[...], w1_ref, w2_ref)
    if not final:
        o_refs[0][...] = y
        return
    ms = jnp.mean(y * y, axis=-1, keepdims=True)
    y = y * lax.rsqrt(ms + EPS) * fw_ref[...]
    lat_ref, ctx_ref = o_refs
    is_lat = pl.program_id(0) < n_lat_tiles

    @pl.when(is_lat)
    def _():
        lat_ref[...] = y

    @pl.when(jnp.logical_not(is_lat))
    def _():
        ctx_ref[...] = y


def _hgrn_in_kernel(*refs, first, n_lat_tiles):
    if first:
        (xs_ref, prow_ref, pcol_ref, xp_ref, mod_ref, nw_ref, w_ref, lbf_ref, lbb_ref, gn_ref,
         q_ref, v_ref, kf_ref, bf_ref, kb_ref, bb_ref, g_ref, x0_ref) = refs
        is_lat = pl.program_id(0) < n_lat_tiles
        prow, pcol = prow_ref[...], pcol_ref[...]
        n_rows, half = prow.shape
        left = jnp.concatenate([jnp.broadcast_to(prow[r:r + 1], (GRID_W, half)) for r in range(n_rows)], axis=0)
        pos = jnp.concatenate([left, jnp.concatenate([pcol] * n_rows, axis=0)], axis=1)
        x = jnp.where(is_lat, xs_ref[...] + pos, xp_ref[...])
        x0_ref[...] = x
    else:
        (x_ref, mod_ref, nw_ref, w_ref, lbf_ref, lbb_ref, gn_ref,
         q_ref, v_ref, kf_ref, bf_ref, kb_ref, bb_ref, g_ref) = refs
        x = x_ref[...]
    tm, d = x.shape
    m = mod_ref[0]
    hn = _norm_mod(x, nw_ref[...], m[0:1], m[1:2]).astype(BF16)

    def proj(j):
        return _dot(hn, w_ref[:, j * d:(j + 1) * d])

    def gate(j, lb_ref, k_ref):
        k = jnp.minimum((1.0 - lb_ref[...]) * _sigmoid(-proj(j)), K_MAX)
        k_ref[...] = k.astype(BF16)
        lg = jnp.log2(1.0 - k)
        terms = []
        for r in range(tm // CHUNK):
            x = lg[r * CHUNK:(r + 1) * CHUNK]
            hi = x.astype(BF16)
            terms.append(jnp.concatenate([hi, (x - hi.astype(F32)).astype(BF16)], axis=0))
        return terms

    def cumulate(terms, b_ref, reverse):
        row = lax.broadcasted_iota(jnp.int32, (CHUNK, CHUNK), 0)
        col = lax.broadcasted_iota(jnp.int32, (CHUNK, CHUNK), 1)
        tri = jnp.where((col >= row) if reverse else (col <= row), 1.0, 0.0).astype(BF16)
        tri2 = jnp.concatenate([tri, tri], axis=1)
        for r, term in enumerate(terms):
            b_ref[r * CHUNK:(r + 1) * CHUNK, :] = _dot(tri2, term)

    terms_f = gate(2, lbf_ref, kf_ref)
    terms_b = gate(3, lbb_ref, kb_ref)
    cumulate(terms_f, bf_ref, False)
    def cols(j, c):
        return _dot(hn, w_ref[:, j * d + c:j * d + c + COL_SLICE])

    for c in range(0, d, COL_SLICE):
        q_ref[:, c:c + COL_SLICE] = (_silu(cols(0, c)) * (HEAD_DIM ** -0.5)).astype(BF16)
        if c == 0:
            cumulate(terms_b, bb_ref, True)
    for c in range(0, d, COL_SLICE):
        g_ref[:, c:c + COL_SLICE] = (gn_ref[:, c:c + COL_SLICE] * _silu(cols(4, c))).astype(BF16)
    v_ref[...] = proj(1).astype(BF16)


def _block_edges(b, reverse, sub):
    n = CHUNK // sub
    edge = 0 if reverse else sub - 1
    far = jnp.concatenate([b[i * sub + edge:i * sub + edge + 1] for i in range(n)], axis=0)
    zero = jnp.zeros_like(far[:1])
    ref = jnp.concatenate([far[1:], zero] if reverse else [zero, far[:-1]], axis=0)
    return far, ref


def _key_order(x, reverse, sub):
    if not reverse:
        return x
    return jnp.concatenate([x[i * sub:(i + 1) * sub] for i in reversed(range(CHUNK // sub))], axis=0)


def _scan_scores(q, k, b, reverse, sub):
    n = CHUNK // sub
    q = q.astype(F32)
    k = k.astype(F32)
    row = lax.broadcasted_iota(jnp.int32, (CHUNK, CHUNK), 0)
    col = lax.broadcasted_iota(jnp.int32, (CHUNK, CHUNK), 1)
    if reverse:
        key = (n - 1 - col // sub) * sub + col % sub
        keep = key >= row
    else:
        keep = col <= row
    far, ref = _block_edges(b, reverse, sub)
    qs, ks = [], []
    for i in range(n):
        rows = slice(i * sub, (i + 1) * sub)
        qs.append((q[rows] * jnp.exp2(b[rows] - ref[i:i + 1])).astype(BF16))
        ks.append(k[rows] * jnp.exp2(far[i:i + 1] - b[rows]))
    blocks = []
    for i in range(n):
        scale = jnp.exp2(ref[i:i + 1] - far)
        sources = range(n - 1, i - 1, -1) if reverse else range(i + 1)
        keys = jnp.concatenate([(ks[j] * scale[j:j + 1]).astype(BF16) for j in sources], axis=0)
        a = _dot_nt(qs[i], keys)
        if len(sources) < n:
            a = jnp.concatenate([a, jnp.zeros((sub, CHUNK - len(sources) * sub), F32)], axis=1)
        blocks.append(a)
    return jnp.where(keep, jnp.concatenate(blocks, axis=0), 0.0)


def _max_block_exponent(b_ref, reverse, sub):
    n = CHUNK // sub
    edge = 0 if reverse else sub - 1
    far = [b_ref[i * sub + edge:i * sub + edge + 1, :] for i in range(n)]
    if reverse:
        worst = -far[n - 1]
        for i in range(n - 1):
            worst = jnp.maximum(worst, far[i + 1] - far[i])
    else:
        worst = -far[0]
        for i in range(1, n):
            worst = jnp.maximum(worst, far[i - 1] - far[i])
    return jnp.max(worst)


def _exact_diag_blocks(q_ref, k_ref, b_ref, a_ref, reverse):
    lane = lax.broadcasted_iota(jnp.int32, (SUB, HEAD_DIM), 1)
    rowi = lax.broadcasted_iota(jnp.int32, (SUB, HEAD_DIM), 0)

    def block(idx, carry):
        h = idx // N_SUB
        i = idx % N_SUB
        r0 = pl.multiple_of(i * SUB, SUB)
        c0 = (N_SUB - 1 - i) * SUB if reverse else r0
        lanes = pl.ds(pl.multiple_of(h * HEAD_DIM, HEAD_DIM), HEAD_DIM)
        qi = q_ref[pl.ds(r0, SUB), lanes].astype(F32)
        ki = k_ref[pl.ds(r0, SUB), lanes].astype(F32)
        bi = b_ref[pl.ds(r0, SUB), lanes]

        def column(s, acc):
            pick = rowi == s
            ks = jnp.sum(jnp.where(pick, ki, 0.0), axis=0, keepdims=True)
            bs = jnp.sum(jnp.where(pick, bi, 0.0), axis=0, keepdims=True)
            w = jnp.exp2(jnp.minimum(bi - bs, 0.0))
            c = jnp.sum(qi * ks * w, axis=-1, keepdims=True)
            return jnp.where(lane == c0 + s, c, acc)

        acc = lax.fori_loop(0, SUB, column, jnp.zeros((SUB, HEAD_DIM), F32))
        keep = (lane - c0 >= rowi) if reverse else (lane - c0 <= rowi)
        in_block = jnp.logical_and(lane >= c0, lane < c0 + SUB)
        old = a_ref[h, pl.ds(r0, SUB), :]
        a_ref[h, pl.ds(r0, SUB), :] = jnp.where(in_block, jnp.where(keep, acc, 0.0), old)
        return carry

    lax.fori_loop(0, HEADS * N_SUB, block, 0)


def _scan_apply(q, k, b, v, a, st_ref, reverse, sub):
    b_last = b[0:1] if reverse else b[CHUNK - 1:CHUNK]
    st = st_ref[...]
    vt = _key_order(v, reverse, sub).T
    lhs = jnp.concatenate([(q.astype(F32) * jnp.exp2(b)).astype(BF16), a.astype(BF16)], axis=1)
    rhs = jnp.concatenate([st.astype(BF16), vt], axis=1)
    k_end = _key_order((k.astype(F32) * jnp.exp2(b_last - b)).astype(BF16), reverse, sub)
    st_ref[...] = st * jnp.exp2(b_last) + _dot(vt, k_end)
    return _dot_nt(lhs, rhs)


def _scan_kernel(*refs, layer, lat_chunks, lat_nc, ctx_nc):
    qf_ref, kf_ref, bf_ref, vf_ref, qb_ref, kb_ref, bb_ref, vb_ref, s0_ref = refs[:9]
    prev_ref = refs[9] if layer > 0 else None
    of_ref, ob_ref, sout_ref, st_ref, a_ref = refs[-5:]
    g = pl.program_id(0)
    is_lat = g < lat_chunks
    c = jnp.where(is_lat, g % lat_nc, (g - lat_chunks) % ctx_nc)
    last = c == jnp.where(is_lat, lat_nc, ctx_nc) - 1

    @pl.when(jnp.logical_and(c == 0, is_lat))
    def _():
        for d in range(2):
            for h in range(HEADS):
                st_ref[d, h] = s0_ref[0, 0, d, h].T

    @pl.when(jnp.logical_and(c == 0, jnp.logical_not(is_lat)))
    def _():
        st_ref[...] = jnp.zeros_like(st_ref)

    dirs = ((qf_ref, kf_ref, bf_ref, vf_ref, of_ref, False), (qb_ref, kb_ref, bb_ref, vb_ref, ob_ref, True))
    head = lambda h: slice(h * HEAD_DIM, (h + 1) * HEAD_DIM)

    def worst_exponent(sub):
        return jnp.maximum(_max_block_exponent(bf_ref, False, sub), _max_block_exponent(bb_ref, True, sub))

    def chunk(sub):
        for h in range(HEADS):
            for d, (q_ref, k_ref, b_ref, _, _, reverse) in enumerate(dirs):
                a_ref[d, h] = _scan_scores(q_ref[:, head(h)], k_ref[:, head(h)], b_ref[:, head(h)], reverse, sub)

        if sub == SUB:
            @pl.when(worst_exponent(SUB) > FAST_EXP_MAX)
            def _():
                for d, (q_ref, k_ref, b_ref, _, _, reverse) in enumerate(dirs):
                    _exact_diag_blocks(q_ref, k_ref, b_ref, a_ref.at[d], reverse)

        for h in range(HEADS):
            for d, (q_ref, k_ref, b_ref, v_ref, o_ref, reverse) in enumerate(dirs):
                o_ref[:, head(h)] = _scan_apply(q_ref[:, head(h)], k_ref[:, head(h)], b_ref[:, head(h)],
                                                v_ref[:, head(h)], a_ref[d, h], st_ref.at[d, h], reverse, sub)

    wide_ok = worst_exponent(SUB_WIDE) <= FAST_EXP_MAX

    @pl.when(wide_ok)
    def _():
        chunk(SUB_WIDE)

    @pl.when(jnp.logical_not(wide_ok))
    def _():
        chunk(SUB)

    @pl.when(jnp.logical_and(last, jnp.logical_not(is_lat)))
    def _():
        for l in range(layer):
            sout_ref[0, l] = prev_ref[0, l]
        for d in range(2):
            for h in range(HEADS):
                sout_ref[0, layer, d, h] = st_ref[d, h].T


def _scan(q, kf, bf, kb, bb, v, s0, prev_states, layer, *, lat_b, lat_t, ctx_b, ctx_t):
    n_tok, d = q.shape
    lat_nc, ctx_nc = lat_t // CHUNK, ctx_t // CHUNK
    lat_chunks = lat_b * lat_nc

    def seq_of(g):
        is_lat = g < lat_chunks
        cs = (g - lat_chunks) // ctx_nc
        start = jnp.where(is_lat, (g // lat_nc) * lat_nc, lat_chunks + cs * ctx_nc)
        return start, jnp.where(is_lat, lat_nc, ctx_nc), jnp.where(is_lat, 0, cs)

    def fwd(g):
        return (g, 0)

    def bwd(g):
        start, nc, _ = seq_of(g)
        return (2 * start + nc - 1 - g, 0)

    blk = lambda im: pl.BlockSpec((CHUNK, d), im)
    state = (2, HEADS, HEAD_DIM, HEAD_DIM)
    ctx_states = lambda n: pl.BlockSpec((1, n) + state, lambda g: (seq_of(g)[2], 0, 0, 0, 0, 0))
    in_specs = [blk(fwd), blk(fwd), blk(fwd), blk(fwd), blk(bwd), blk(bwd), blk(bwd), blk(bwd),
                pl.BlockSpec((1, 1) + state, lambda g: (jnp.minimum(g // lat_nc, lat_b - 1), layer, 0, 0, 0, 0))]
    args = [q, kf, bf, v, q, kb, bb, v, s0]
    if layer > 0:
        in_specs.append(ctx_states(layer))
        args.append(prev_states)
    return pl.pallas_call(
        functools.partial(_scan_kernel, layer=layer, lat_chunks=lat_chunks, lat_nc=lat_nc, ctx_nc=ctx_nc),
        grid=(n_tok // CHUNK,),
        in_specs=in_specs,
        out_specs=[blk(fwd), blk(bwd), ctx_states(layer + 1)],
        out_shape=[jax.ShapeDtypeStruct((n_tok, d), F32), jax.ShapeDtypeStruct((n_tok, d), F32),
                   jax.ShapeDtypeStruct((ctx_b, layer + 1) + state, F32)],
        scratch_shapes=[pltpu.VMEM((2, HEADS, HEAD_DIM, HEAD_DIM), F32),
                        pltpu.VMEM((2, HEADS, CHUNK, CHUNK), F32)],
        compiler_params=_params("arbitrary"),
        name="hgrn_scan",
    )(*args)


def _hgrn_out_mlp_kernel(of_ref, ob_ref, g_ref, x_ref, mod_ref, w_ref, nw_ref, w1_ref, w2_ref, o_ref):
    m = mod_ref[0]
    o = of_ref[...] + ob_ref[...]
    parts = []
    for h in range(HEADS):
        oh = o[:, h * HEAD_DIM:(h + 1) * HEAD_DIM]
        ms = jnp.mean(oh * oh, axis=-1, keepdims=True)
        parts.append(oh * lax.rsqrt(ms + EPS))
    on = jnp.concatenate(parts, axis=1) * g_ref[...].astype(F32)
    x = x_ref[...] + m[2:3] * _dot(on.astype(BF16), w_ref[...])
    o_ref[...] = _mlp_stage(x, m, nw_ref[...], w1_ref, w2_ref)


def _conv_in_kernel(x_ref, mod_ref, nw_ref, w_ref, u_ref):
    d = x_ref.shape[1]
    m = mod_ref[0]
    hn = _norm_mod(x_ref[...], nw_ref[...], m[0:1], m[1:2]).astype(BF16)
    for c in range(0, d, COL_SLICE):
        a = _dot(hn, w_ref[:, c:c + COL_SLICE])
        gate = _dot(hn, w_ref[:, d + c:d + c + COL_SLICE])
        u_ref[:, c:c + COL_SLICE] = a * _sigmoid(gate)


def _conv_out_kernel(up_ref, u_ref, un_ref, x_ref, mod_ref, wdw_ref, bdw_ref, lng_ref, lnb_ref, w_ref,
                     o_ref, ext_ref, sh_ref, wb_ref, cv_ref, *, n_lat_tiles, tiles_per_lat_seq):
    i = pl.program_id(0)
    tm, d = u_ref.shape
    pos = i % tiles_per_lat_seq
    is_lat = i < n_lat_tiles
    first = jnp.logical_or(jnp.logical_not(is_lat), pos == 0)
    last = jnp.logical_or(jnp.logical_not(is_lat), pos == tiles_per_lat_seq - 1)
    ext_ref[0:HALO, :] = jnp.where(first, 0.0, up_ref[...])
    ext_ref[HALO:HALO + tm, :] = u_ref[...]
    ext_ref[HALO + tm:HALO + tm + HALO, :] = jnp.where(last, 0.0, un_ref[...])

    n_sh = sh_ref.shape[1]
    for s in range(1, SUBLANES):
        sh_ref[s - 1] = ext_ref[s:s + n_sh, :]

    @pl.when(i == 0)
    def _():
        for t in range(CONV_K):
            wb_ref[t] = jnp.broadcast_to(wdw_ref[t:t + 1, :], (SUBLANES, d))

    groups, lanes = 8, 128
    rows = groups * SUBLANES
    off = HALO - CONV_K // 2
    n_a = -(-(CONV_K + off) // SUBLANES)

    def body(r, carry):
        base = pl.multiple_of(r * rows, rows)
        for lc in range(d // lanes):
            ls = slice(lc * lanes, (lc + 1) * lanes)
            accs = [jnp.broadcast_to(bdw_ref[:, ls], (SUBLANES, lanes))] * groups
            for s in range(SUBLANES):
                taps = [(a, a * SUBLANES + s - off) for a in range(n_a)
                        if 0 <= a * SUBLANES + s - off < CONV_K]
                span = rows + max(a for a, _ in taps) * SUBLANES
                if s == 0:
                    blk = ext_ref[pl.ds(base, span), ls]
                else:
                    blk = sh_ref[s - 1, pl.ds(base, span), ls]
                for a, t in taps:
                    w = wb_ref[t, :, ls]
                    accs = [acc + w * blk[(gi + a) * SUBLANES:(gi + a + 1) * SUBLANES]
                            for gi, acc in enumerate(accs)]
            cv_ref[pl.ds(base, rows), ls] = jnp.concatenate(accs, axis=0)
        return carry

    lax.fori_loop(0, tm // rows, body, 0)

    u = cv_ref[...]
    mu = jnp.mean(u, axis=-1, keepdims=True)
    uc = u - mu
    rstd = lax.rsqrt(jnp.mean(uc * uc, axis=-1, keepdims=True) + EPS)
    out = None
    for c in range(0, d, K_SLICE):
        cs = slice(c, c + K_SLICE)
        y = uc[:, cs] * rstd * lng_ref[:, cs] + lnb_ref[:, cs]
        part = _dot(_silu(y).astype(BF16), w_ref[cs, :])
        out = part if out is None else out + part
    o_ref[...] = x_ref[...] + mod_ref[0][2:3] * out


def _grid_pos_tables(n_tok, dim):
    nf = dim // 4
    omega = 1.0 / (10000.0 ** (jnp.arange(nf, dtype=F32) / nf))
    er = jnp.arange(n_tok // GRID_W, dtype=F32)[:, None] * omega
    ec = jnp.arange(GRID_W, dtype=F32)[:, None] * omega
    return (jnp.concatenate([jnp.sin(er), jnp.cos(er)], axis=-1),
            jnp.concatenate([jnp.sin(ec), jnp.cos(ec)], axis=-1))


def _lower_bounds(p):
    p = jax.nn.softmax(p.astype(F32), axis=0)
    return jnp.maximum(jnp.cumsum(p, axis=0) - p[0], 0.0)


def kernel(x_prompt, x_sample, c, state_hgrn, c_ctx, w_mod, b_mod, norm_mix, norm_mlp, hgrn_w_in, hgrn_lb_fwd, hgrn_lb_bwd, hgrn_g_norm, hgrn_w_out, conv_w_pw1, conv_w_dw, conv_b_dw, conv_ln_g, conv_ln_b, conv_w_pw2, mlp_w1, mlp_w2, final_norm):
    ctx_b, ctx_t, d = x_prompt.shape
    lat_b, lat_t, _ = x_sample.shape
    depth = w_mod.shape[0]
    d_ff = mlp_w1.shape[2]
    n_lat, n_ctx = lat_b * lat_t, ctx_b * ctx_t
    n_tok = n_lat + n_ctx
    assert d == HEADS * HEAD_DIM and lat_b < MOD_ROWS
    assert depth % 2 == 0
    assert lat_t % TM == 0 and n_ctx % TM == 0 and ctx_t % CHUNK == 0 and ctx_t == TM_CONV and d_ff % TF == 0
    assert TM % GRID_W == 0 and (TM // GRID_W) % SUBLANES == 0

    def mod_row(tm):
        return lambda i, *_: (jnp.where(i * tm < n_lat, (i * tm) // lat_t, lat_b), 0, 0)

    def mod_spec(tm):
        return pl.BlockSpec((1, N_MOD, d), mod_row(tm))

    def tok_spec(tm, width=d):
        return pl.BlockSpec((tm, width), lambda i, *_: (i, 0))

    def full_spec(shape):
        return pl.BlockSpec(shape, lambda i, *_: (0,) * len(shape))

    def resident_spec(shape):
        return pl.BlockSpec(shape, lambda i, *_: (0,) * len(shape), pipeline_mode=pl.Buffered(1))

    row = lambda a: a.reshape(1, -1)
    tok_f32 = jax.ShapeDtypeStruct((n_tok, d), F32)
    tok_bf16 = jax.ShapeDtypeStruct((n_tok, d), BF16)

    cvec = jnp.zeros((MOD_ROWS, d), F32).at[:lat_b].set(c).at[lat_b].set(c_ctx)
    mods = _modulation(cvec, w_mod, b_mod).reshape(depth, MOD_ROWS, N_MOD, d)

    lbs_f = _lower_bounds(hgrn_lb_fwd)
    lbs_b = _lower_bounds(hgrn_lb_bwd)

    pos_row, pos_col = _grid_pos_tables(lat_t, d)
    n_lat_tiles = n_lat // TM
    tiles_per_lat_seq = lat_t // TM
    first_specs = [pl.BlockSpec((TM, d), lambda i: (jnp.minimum(i, n_lat_tiles - 1), 0)),
                   pl.BlockSpec((TM // GRID_W, d // 2), lambda i: (i % tiles_per_lat_seq, 0)),
                   full_spec((GRID_W, d // 2)),
                   pl.BlockSpec((TM, d), lambda i: (jnp.maximum(i - n_lat_tiles, 0), 0))]
    x = None

    new_states = None
    for i in range(depth):
        mod = mods[i]
        if i % 2 == 0:
            a = i // 2
            first = i == 0
            x_args = ((x_sample.reshape(n_lat, d), pos_row, pos_col, x_prompt.reshape(n_ctx, d))
                      if first else (x,))
            outs = pl.pallas_call(
                functools.partial(_hgrn_in_kernel, first=first, n_lat_tiles=n_lat_tiles),
                grid=(n_tok // TM,),
                in_specs=(first_specs if first else [tok_spec(TM)]) + [
                    mod_spec(TM), full_spec((1, d)), resident_spec((d, 5 * d)),
                    full_spec((1, d)), full_spec((1, d)), full_spec((1, d))],
                out_specs=[tok_spec(TM)] * (8 if first else 7),
                out_shape=[tok_bf16, tok_bf16, tok_bf16, tok_f32, tok_bf16, tok_f32, tok_bf16]
                + ([tok_f32] if first else []),
                compiler_params=_params("arbitrary"),
                name="hgrn_in",
            )(*x_args, mod, row(norm_mix[i]), hgrn_w_in[a].astype(BF16), row(lbs_f[a]), row(lbs_b[a]),
              row(hgrn_g_norm[a]))
            q, v, kf, bf, kb, bb, g = outs[:7]
            if first:
                x = outs[7]
            of, ob, new_states = _scan(q, kf, bf, kb, bb, v, state_hgrn, new_states, a, lat_b=lat_b, lat_t=lat_t,
                                       ctx_b=ctx_b, ctx_t=ctx_t)
            x = pl.pallas_call(
                _hgrn_out_mlp_kernel,
                grid=(n_tok // TM,),
                in_specs=[tok_spec(TM), tok_spec(TM), tok_spec(TM), tok_spec(TM), mod_spec(TM),
                          full_spec((d, d)), full_spec((1, d)), resident_spec((d, d_ff)),
                          resident_spec((d_ff, d))],
                out_specs=tok_spec(TM),
                out_shape=tok_f32,
                compiler_params=_params("parallel"),
                name="hgrn_out_mlp",
            )(of, ob, g, x, mod, hgrn_w_out[a].astype(BF16), row(norm_mlp[i]), mlp_w1[i].astype(BF16),
              mlp_w2[i].astype(BF16))
            continue

        b = i // 2
        u = pl.pallas_call(
            _conv_in_kernel,
            grid=(n_tok // TM,),
            in_specs=[tok_spec(TM), mod_spec(TM), full_spec((1, d)), full_spec((d, 2 * d))],
            out_specs=tok_spec(TM),
            out_shape=tok_f32,
            compiler_params=_params("parallel"),
            name="conv_in",
        )(x, mod, row(norm_mix[i]), conv_w_pw1[b].astype(BF16))
        tm = TM_CONV
        hb = tm // HALO
        n_halo = n_tok // HALO
        wdw = jnp.zeros((CONV_K + 1, d), F32).at[:CONV_K].set(conv_w_dw[b])
        x = pl.pallas_call(
            functools.partial(_conv_out_kernel, n_lat_tiles=n_lat // tm, tiles_per_lat_seq=lat_t // tm),
            grid=(n_tok // tm,),
            in_specs=[pl.BlockSpec((HALO, d), lambda i: (jnp.maximum(i * hb - 1, 0), 0)),
                      tok_spec(tm),
                      pl.BlockSpec((HALO, d), lambda i: (jnp.minimum((i + 1) * hb, n_halo - 1), 0)),
                      tok_spec(tm), mod_spec(tm), full_spec((CONV_K + 1, d)), full_spec((1, d)),
                      full_spec((1, d)), full_spec((1, d)), full_spec((d, d))],
            out_specs=tok_spec(tm),
            out_shape=tok_f32,
            scratch_shapes=[pltpu.VMEM((tm + 2 * HALO, d), F32),
                            pltpu.VMEM((SUBLANES - 1, tm + 2 * HALO - SUBLANES, d), F32),
                            pltpu.VMEM((CONV_K, SUBLANES, d), F32),
                            pltpu.VMEM((tm, d), F32)],
            compiler_params=_params("arbitrary"),
            name="conv_out",
        )(u, u, u, x, mod, wdw, row(conv_b_dw[b]), row(conv_ln_g[b]), row(conv_ln_b[b]),
          conv_w_pw2[b].astype(BF16))

        final = i == depth - 1
        if final:
            out_specs = [pl.BlockSpec((TM, d), lambda i: (jnp.minimum(i, n_lat_tiles - 1), 0)),
                         pl.BlockSpec((TM, d), lambda i: (jnp.maximum(i - n_lat_tiles, 0), 0))]
            out_shape = [jax.ShapeDtypeStruct((n_lat, d), F32), jax.ShapeDtypeStruct((n_ctx, d), F32)]
        else:
            out_specs, out_shape = tok_spec(TM), tok_f32
        x = pl.pallas_call(
            functools.partial(_mlp_kernel, final=final, n_lat_tiles=n_lat_tiles),
            grid=(n_tok // TM,),
            in_specs=[tok_spec(TM), mod_spec(TM), full_spec((1, d)), full_spec((1, d)),
                      resident_spec((d, d_ff)), resident_spec((d_ff, d))],
            out_specs=out_specs,
            out_shape=out_shape,
            compiler_params=_params("arbitrary"),
            name="mlp",
        )(x, mod, row(norm_mlp[i]), row(final_norm), mlp_w1[i].astype(BF16), mlp_w2[i].astype(BF16))

    y_lat, y_ctx = x
    y_sample = y_lat.reshape(lat_b, lat_t, d)
    y_prompt = y_ctx.reshape(ctx_b, ctx_t, d)
    return (y_prompt, y_sample, new_states.astype(x_prompt.dtype))
```

```python
import functools

import jax
import jax.numpy as jnp
from jax import lax
from jax.experimental import pallas as pl
from jax.experimental.pallas import tpu as pltpu

F32 = jnp.float32
BF16 = jnp.bfloat16

N_MOD = 6
HEADS = 8
HEAD_DIM = 128
GRID_W = 64
CONV_K = 31
EPS = 1e-6
K_MAX = 1.0 - 1e-6

CHUNK = 128
STEP_CHUNKS = 2
SUB = 16
SUB_WIDE = 32
N_SUB = CHUNK // SUB
FAST_EXP_MAX = 118.0
SUBLANES = 8
HALO = 16
MOD_ROWS = 8

TM = 512
TM_CONV = 256
TF = 1024
COL_SLICE = 256
K_SLICE = 256
VMEM_LIMIT = 56 * 1024 * 1024


def _params(*sem):
    return pltpu.CompilerParams(dimension_semantics=sem, vmem_limit_bytes=VMEM_LIMIT)


def _dot(a, b):
    return jnp.dot(a, b, preferred_element_type=F32)


def _dot_nt(a, b):
    return lax.dot_general(a, b, (((1,), (1,)), ((), ())), preferred_element_type=F32)


def _sigmoid(x):
    return 1.0 / (1.0 + jnp.exp(-x))


def _silu(x):
    return x * _sigmoid(x)


def _norm_mod(x, nw, shift, scale):
    ms = jnp.mean(x * x, axis=-1, keepdims=True)
    y = x * lax.rsqrt(ms + EPS) * nw
    return y * (1.0 + scale) + shift


def _mod_kernel(c_ref, w_ref, b_ref, o_ref):
    c = _silu(c_ref[...]).astype(BF16)
    o_ref[0] = _dot(c, w_ref[0].astype(BF16)) + b_ref[0]


def _modulation(cvec, w_mod, b_mod):
    depth, d, n = w_mod.shape
    tn = 1536
    return pl.pallas_call(
        _mod_kernel,
        grid=(depth, n // tn),
        in_specs=[
            pl.BlockSpec((MOD_ROWS, d), lambda l, j: (0, 0)),
            pl.BlockSpec((1, d, tn), lambda l, j: (l, 0, j)),
            pl.BlockSpec((1, 1, tn), lambda l, j: (l, 0, j)),
        ],
        out_specs=pl.BlockSpec((1, MOD_ROWS, tn), lambda l, j: (l, 0, j)),
        out_shape=jax.ShapeDtypeStruct((depth, MOD_ROWS, n), F32),
        compiler_params=_params("parallel", "parallel"),
        name="modulation",
    )(cvec, w_mod, b_mod.reshape(depth, 1, n))


def _mlp_stage(x, m, nw, w1_ref, w2_ref):
    hn = _norm_mod(x, nw, m[3:4], m[4:5]).astype(BF16)
    acc = None
    for j in range(w1_ref.shape[1] // TF):
        h = _dot(hn, w1_ref[:, j * TF:(j + 1) * TF])
        h = jnp.square(jnp.maximum(h, 0.0)).astype(BF16)
        part = _dot(h, w2_ref[j * TF:(j + 1) * TF, :])
        acc = part if acc is None else acc + part
    return x + m[5:6] * acc


def _mlp_kernel(x_ref, mod_ref, nw_ref, fw_ref, w1_ref, w2_ref, *o_refs, final, n_lat_tiles):
    y = _mlp_stage(x_ref[...], mod_ref[0], nw_ref[...], w1_ref, w2_ref)
    if not final:
        o_refs[0][...] = y
        return
    ms = jnp.mean(y * y, axis=-1, keepdims=True)
    y = y * lax.rsqrt(ms + EPS) * fw_ref[...]
    lat_ref, ctx_ref = o_refs
    is_lat = pl.program_id(0) < n_lat_tiles

    @pl.when(is_lat)
    def _():
        lat_ref[...] = y

    @pl.when(jnp.logical_not(is_lat))
    def _():
        ctx_ref[...] = y


def _hgrn_in_kernel(*refs, first, n_lat_tiles):
    if first:
        (xs_ref, prow_ref, pcol_ref, xp_ref, mod_ref, nw_ref, w_ref, lbf_ref, lbb_ref, gn_ref,
         q_ref, v_ref, kf_ref, bf_ref, kb_ref, bb_ref, g_ref, x0_ref) = refs
        is_lat = pl.program_id(0) < n_lat_tiles
        prow, pcol = prow_ref[...], pcol_ref[...]
        n_rows, half = prow.shape
        left = jnp.concatenate([jnp.broadcast_to(prow[r:r + 1], (GRID_W, half)) for r in range(n_rows)], axis=0)
        pos = jnp.concatenate([left, jnp.concatenate([pcol] * n_rows, axis=0)], axis=1)
        x = jnp.where(is_lat, xs_ref[...] + pos, xp_ref[...])
        x0_ref[...] = x
    else:
        (x_ref, mod_ref, nw_ref, w_ref, lbf_ref, lbb_ref, gn_ref,
         q_ref, v_ref, kf_ref, bf_ref, kb_ref, bb_ref, g_ref) = refs
        x = x_ref[...]
    tm, d = x.shape
    m = mod_ref[0]
    hn = _norm_mod(x, nw_ref[...], m[0:1], m[1:2]).astype(BF16)

    def proj(j):
        return _dot(hn, w_ref[:, j * d:(j + 1) * d])

    def gate(j, lb_ref, k_ref):
        k = jnp.minimum((1.0 - lb_ref[...]) * _sigmoid(-proj(j)), K_MAX)
        k_ref[...] = k.astype(BF16)
        lg = jnp.log2(1.0 - k)
        terms = []
        for r in range(tm // CHUNK):
            x = lg[r * CHUNK:(r + 1) * CHUNK]
            hi = x.astype(BF16)
            terms.append(jnp.concatenate([hi, (x - hi.astype(F32)).astype(BF16)], axis=0))
        return terms

    def cumulate(terms, b_ref, reverse):
        row = lax.broadcasted_iota(jnp.int32, (CHUNK, CHUNK), 0)
        col = lax.broadcasted_iota(jnp.int32, (CHUNK, CHUNK), 1)
        tri = jnp.where((col >= row) if reverse else (col <= row), 1.0, 0.0).astype(BF16)
        tri2 = jnp.concatenate([tri, tri], axis=1)
        for r, term in enumerate(terms):
            b_ref[r * CHUNK:(r + 1) * CHUNK, :] = _dot(tri2, term)

    terms_f = gate(2, lbf_ref, kf_ref)
    terms_b = gate(3, lbb_ref, kb_ref)
    cumulate(terms_f, bf_ref, False)
    def cols(j, c):
        return _dot(hn, w_ref[:, j * d + c:j * d + c + COL_SLICE])

    for c in range(0, d, COL_SLICE):
        q_ref[:, c:c + COL_SLICE] = (_silu(cols(0, c)) * (HEAD_DIM ** -0.5)).astype(BF16)
        if c == 0:
            cumulate(terms_b, bb_ref, True)
    for c in range(0, d, COL_SLICE):
        g_ref[:, c:c + COL_SLICE] = (gn_ref[:, c:c + COL_SLICE] * _silu(cols(4, c))).astype(BF16)
    v_ref[...] = proj(1).astype(BF16)


def _block_edges(b, reverse, sub):
    n = CHUNK // sub
    edge = 0 if reverse else sub - 1
    far = jnp.concatenate([b[i * sub + edge:i * sub + edge + 1] for i in range(n)], axis=0)
    zero = jnp.zeros_like(far[:1])
    ref = jnp.concatenate([far[1:], zero] if reverse else [zero, far[:-1]], axis=0)
    return far, ref


def _key_order(x, reverse, sub):
    if not reverse:
        return x
    return jnp.concatenate([x[i * sub:(i + 1) * sub] for i in reversed(range(CHUNK // sub))], axis=0)


def _scan_scores(q, k, b, reverse, sub):
    n = CHUNK // sub
    q = q.astype(F32)
    k = k.astype(F32)
    row = lax.broadcasted_iota(jnp.int32, (CHUNK, CHUNK), 0)
    col = lax.broadcasted_iota(jnp.int32, (CHUNK, CHUNK), 1)
    if reverse:
        key = (n - 1 - col // sub) * sub + col % sub
        keep = key >= row
    else:
        keep = col <= row
    far, ref = _block_edges(b, reverse, sub)
    qs, ks = [], []
    for i in range(n):
        rows = slice(i * sub, (i + 1) * sub)
        qs.append((q[rows] * jnp.exp2(b[rows] - ref[i:i + 1])).astype(BF16))
        ks.append(k[rows] * jnp.exp2(far[i:i + 1] - b[rows]))
    blocks = []
    for i in range(n):
        scale = jnp.exp2(ref[i:i + 1] - far)
        sources = range(n - 1, i - 1, -1) if reverse else range(i + 1)
        keys = jnp.concatenate([(ks[j] * scale[j:j + 1]).astype(BF16) for j in sources], axis=0)
        a = _dot_nt(qs[i], keys)
        if len(sources) < n:
            a = jnp.concatenate([a, jnp.zeros((sub, CHUNK - len(sources) * sub), F32)], axis=1)
        blocks.append(a)
    return jnp.where(keep, jnp.concatenate(blocks, axis=0), 0.0)


def _max_block_exponent(b_ref, reverse, sub):
    n = CHUNK // sub
    edge = 0 if reverse else sub - 1
    far = [b_ref[i * sub + edge:i * sub + edge + 1, :] for i in range(n)]
    if reverse:
        worst = -far[n - 1]
        for i in range(n - 1):
            worst = jnp.maximum(worst, far[i + 1] - far[i])
    else:
        worst = -far[0]
        for i in range(1, n):
            worst = jnp.maximum(worst, far[i - 1] - far[i])
    return jnp.max(worst)


def _exact_diag_blocks(q_ref, k_ref, b_ref, a_ref, reverse):
    lane = lax.broadcasted_iota(jnp.int32, (SUB, HEAD_DIM), 1)
    rowi = lax.broadcasted_iota(jnp.int32, (SUB, HEAD_DIM), 0)

    def block(idx, carry):
        h = idx // N_SUB
        i = idx % N_SUB
        r0 = pl.multiple_of(i * SUB, SUB)
        c0 = (N_SUB - 1 - i) * SUB if reverse else r0
        lanes = pl.ds(pl.multiple_of(h * HEAD_DIM, HEAD_DIM), HEAD_DIM)
        qi = q_ref[pl.ds(r0, SUB), lanes].astype(F32)
        ki = k_ref[pl.ds(r0, SUB), lanes].astype(F32)
        bi = b_ref[pl.ds(r0, SUB), lanes]

        def column(s, acc):
            pick = rowi == s
            ks = jnp.sum(jnp.where(pick, ki, 0.0), axis=0, keepdims=True)
            bs = jnp.sum(jnp.where(pick, bi, 0.0), axis=0, keepdims=True)
            w = jnp.exp2(jnp.minimum(bi - bs, 0.0))
            c = jnp.sum(qi * ks * w, axis=-1, keepdims=True)
            return jnp.where(lane == c0 + s, c, acc)

        acc = lax.fori_loop(0, SUB, column, jnp.zeros((SUB, HEAD_DIM), F32))
        keep = (lane - c0 >= rowi) if reverse else (lane - c0 <= rowi)
        in_block = jnp.logical_and(lane >= c0, lane < c0 + SUB)
        old = a_ref[h, pl.ds(r0, SUB), :]
        a_ref[h, pl.ds(r0, SUB), :] = jnp.where(in_block, jnp.where(keep, acc, 0.0), old)
        return carry

    lax.fori_loop(0, HEADS * N_SUB, block, 0)


def _scan_apply(q, k, b, v, a, st_ref, reverse, sub):
    b_last = b[0:1] if reverse else b[CHUNK - 1:CHUNK]
    st = st_ref[...]
    vt = _key_order(v, reverse, sub).T
    lhs = jnp.concatenate([(q.astype(F32) * jnp.exp2(b)).astype(BF16), a.astype(BF16)], axis=1)
    rhs = jnp.concatenate([st.astype(BF16), vt], axis=1)
    k_end = _key_order((k.astype(F32) * jnp.exp2(b_last - b)).astype(BF16), reverse, sub)
    st_ref[...] = st * jnp.exp2(b_last) + _dot(vt, k_end)
    return _dot_nt(lhs, rhs)


def _scan_kernel(*refs, layer, lat_chunks, lat_nc, ctx_nc):
    qf_ref, kf_ref, bf_ref, vf_ref, qb_ref, kb_ref, bb_ref, vb_ref, s0_ref = refs[:9]
    prev_ref = refs[9] if layer > 0 else None
    of_ref, ob_ref, sout_ref, st_ref, a_ref = refs[-5:]
    g = pl.program_id(0)
    is_lat = g < lat_chunks
    c = jnp.where(is_lat, g % lat_nc, (g - lat_chunks) % ctx_nc)
    last = c == jnp.where(is_lat, lat_nc, ctx_nc) - 1

    @pl.when(jnp.logical_and(c == 0, is_lat))
    def _():
        for d in range(2):
            for h in range(HEADS):
                st_ref[d, h] = s0_ref[0, 0, d, h].T

    @pl.when(jnp.logical_and(c == 0, jnp.logical_not(is_lat)))
    def _():
        st_ref[...] = jnp.zeros_like(st_ref)

    head = lambda h: slice(h * HEAD_DIM, (h + 1) * HEAD_DIM)

    def run_chunk(dirs):
        def worst_exponent(sub):
            return jnp.maximum(_max_block_exponent(dirs[0][2], False, sub), _max_block_exponent(dirs[1][2], True, sub))

        def chunk(sub):
            for h in range(HEADS):
                for d, (q_ref, k_ref, b_ref, _, _, reverse) in enumerate(dirs):
                    a_ref[d, h] = _scan_scores(q_ref[:, head(h)], k_ref[:, head(h)], b_ref[:, head(h)], reverse, sub)

            if sub == SUB:
                @pl.when(worst_exponent(SUB) > FAST_EXP_MAX)
                def _():
                    for d, (q_ref, k_ref, b_ref, _, _, reverse) in enumerate(dirs):
                        _exact_diag_blocks(q_ref, k_ref, b_ref, a_ref.at[d], reverse)

            for h in range(HEADS):
                for d, (q_ref, k_ref, b_ref, v_ref, o_ref, reverse) in enumerate(dirs):
                    o_ref[:, head(h)] = _scan_apply(q_ref[:, head(h)], k_ref[:, head(h)], b_ref[:, head(h)],
                                                    v_ref[:, head(h)], a_ref[d, h], st_ref.at[d, h], reverse, sub)

        wide_ok = worst_exponent(SUB_WIDE) <= FAST_EXP_MAX
        pl.when(wide_ok)(functools.partial(chunk, SUB_WIDE))
        pl.when(jnp.logical_not(wide_ok))(functools.partial(chunk, SUB))

    for cc in range(STEP_CHUNKS):
        fr = pl.ds(cc * CHUNK, CHUNK)
        br = pl.ds((STEP_CHUNKS - 1 - cc) * CHUNK, CHUNK)
        run_chunk((tuple(r.at[fr] for r in (qf_ref, kf_ref, bf_ref, vf_ref, of_ref)) + (False,),
                   tuple(r.at[br] for r in (qb_ref, kb_ref, bb_ref, vb_ref, ob_ref)) + (True,)))

    @pl.when(jnp.logical_and(last, jnp.logical_not(is_lat)))
    def _():
        for l in range(layer):
            sout_ref[0, l] = prev_ref[0, l]
        for d in range(2):
            for h in range(HEADS):
                sout_ref[0, layer, d, h] = st_ref[d, h].T


def _scan(q, kf, bf, kb, bb, v, s0, prev_states, layer, *, lat_b, lat_t, ctx_b, ctx_t):
    n_tok, d = q.shape
    step = STEP_CHUNKS * CHUNK
    lat_nc, ctx_nc = lat_t // step, ctx_t // step
    lat_chunks = lat_b * lat_nc

    def seq_of(g):
        is_lat = g < lat_chunks
        cs = (g - lat_chunks) // ctx_nc
        start = jnp.where(is_lat, (g // lat_nc) * lat_nc, lat_chunks + cs * ctx_nc)
        return start, jnp.where(is_lat, lat_nc, ctx_nc), jnp.where(is_lat, 0, cs)

    def fwd(g):
        return (g, 0)

    def bwd(g):
        start, nc, _ = seq_of(g)
        return (2 * start + nc - 1 - g, 0)

    blk = lambda im: pl.BlockSpec((step, d), im)
    state = (2, HEADS, HEAD_DIM, HEAD_DIM)
    ctx_states = lambda n: pl.BlockSpec((1, n) + state, lambda g: (seq_of(g)[2], 0, 0, 0, 0, 0))
    in_specs = [blk(fwd), blk(fwd), blk(fwd), blk(fwd), blk(bwd), blk(bwd), blk(bwd), blk(bwd),
                pl.BlockSpec((1, 1) + state, lambda g: (jnp.minimum(g // lat_nc, lat_b - 1), layer, 0, 0, 0, 0))]
    args = [q, kf, bf, v, q, kb, bb, v, s0]
    if layer > 0:
        in_specs.append(ctx_states(layer))
        args.append(prev_states)
    return pl.pallas_call(
        functools.partial(_scan_kernel, layer=layer, lat_chunks=lat_chunks, lat_nc=lat_nc, ctx_nc=ctx_nc),
        grid=(n_tok // step,),
        in_specs=in_specs,
        out_specs=[blk(fwd), blk(bwd), ctx_states(layer + 1)],
        out_shape=[jax.ShapeDtypeStruct((n_tok, d), F32), jax.ShapeDtypeStruct((n_tok, d), F32),
                   jax.ShapeDtypeStruct((ctx_b, layer + 1) + state, F32)],
        scratch_shapes=[pltpu.VMEM((2, HEADS, HEAD_DIM, HEAD_DIM), F32),
                        pltpu.VMEM((2, HEADS, CHUNK, CHUNK), F32)],
        compiler_params=_params("arbitrary"),
        name="hgrn_scan",
    )(*args)


def _hgrn_out_mlp_kernel(of_ref, ob_ref, g_ref, x_ref, mod_ref, w_ref, nw_ref, w1_ref, w2_ref, o_ref):
    m = mod_ref[0]
    o = of_ref[...] + ob_ref[...]
    parts = []
    for h in range(HEADS):
        oh = o[:, h * HEAD_DIM:(h + 1) * HEAD_DIM]
        ms = jnp.mean(oh * oh, axis=-1, keepdims=True)
        parts.append(oh * lax.rsqrt(ms + EPS))
    on = jnp.concatenate(parts, axis=1) * g_ref[...].astype(F32)
    x = x_ref[...] + m[2:3] * _dot(on.astype(BF16), w_ref[...])
    o_ref[...] = _mlp_stage(x, m, nw_ref[...], w1_ref, w2_ref)


def _conv_in_kernel(x_ref, mod_ref, nw_ref, w_ref, u_ref):
    d = x_ref.shape[1]
    m = mod_ref[0]
    hn = _norm_mod(x_ref[...], nw_ref[...], m[0:1], m[1:2]).astype(BF16)
    for c in range(0, d, COL_SLICE):
        a = _dot(hn, w_ref[:, c:c + COL_SLICE])
        gate = _dot(hn, w_ref[:, d + c:d + c + COL_SLICE])
        u_ref[:, c:c + COL_SLICE] = a * _sigmoid(gate)


def _conv_out_kernel(up_ref, u_ref, un_ref, x_ref, mod_ref, wdw_ref, bdw_ref, lng_ref, lnb_ref, w_ref,
                     o_ref, ext_ref, sh_ref, wb_ref, cv_ref, *, n_lat_tiles, tiles_per_lat_seq):
    i = pl.program_id(0)
    tm, d = u_ref.shape
    pos = i % tiles_per_lat_seq
    is_lat = i < n_lat_tiles
    first = jnp.logical_or(jnp.logical_not(is_lat), pos == 0)
    last = jnp.logical_or(jnp.logical_not(is_lat), pos == tiles_per_lat_seq - 1)
    ext_ref[0:HALO, :] = jnp.where(first, 0.0, up_ref[...])
    ext_ref[HALO:HALO + tm, :] = u_ref[...]
    ext_ref[HALO + tm:HALO + tm + HALO, :] = jnp.where(last, 0.0, un_ref[...])

    n_sh = sh_ref.shape[1]
    for s in range(1, SUBLANES):
        sh_ref[s - 1] = ext_ref[s:s + n_sh, :]

    @pl.when(i == 0)
    def _():
        for t in range(CONV_K):
            wb_ref[t] = jnp.broadcast_to(wdw_ref[t:t + 1, :], (SUBLANES, d))

    groups, lanes = 8, 128
    rows = groups * SUBLANES
    off = HALO - CONV_K // 2
    n_a = -(-(CONV_K + off) // SUBLANES)

    def body(r, carry):
        base = pl.multiple_of(r * rows, rows)
        for lc in range(d // lanes):
            ls = slice(lc * lanes, (lc + 1) * lanes)
            accs = [jnp.broadcast_to(bdw_ref[:, ls], (SUBLANES, lanes))] * groups
            for s in range(SUBLANES):
                taps = [(a, a * SUBLANES + s - off) for a in range(n_a)
                        if 0 <= a * SUBLANES + s - off < CONV_K]
                span = rows + max(a for a, _ in taps) * SUBLANES
                if s == 0:
                    blk = ext_ref[pl.ds(base, span), ls]
                else:
                    blk = sh_ref[s - 1, pl.ds(base, span), ls]
                for a, t in taps:
                    w = wb_ref[t, :, ls]
                    accs = [acc + w * blk[(gi + a) * SUBLANES:(gi + a + 1) * SUBLANES]
                            for gi, acc in enumerate(accs)]
            cv_ref[pl.ds(base, rows), ls] = jnp.concatenate(accs, axis=0)
        return carry

    lax.fori_loop(0, tm // rows, body, 0)

    u = cv_ref[...]
    mu = jnp.mean(u, axis=-1, keepdims=True)
    uc = u - mu
    rstd = lax.rsqrt(jnp.mean(uc * uc, axis=-1, keepdims=True) + EPS)
    out = None
    for c in range(0, d, K_SLICE):
        cs = slice(c, c + K_SLICE)
        y = uc[:, cs] * rstd * lng_ref[:, cs] + lnb_ref[:, cs]
        part = _dot(_silu(y).astype(BF16), w_ref[cs, :])
        out = part if out is None else out + part
    o_ref[...] = x_ref[...] + mod_ref[0][2:3] * out


def _grid_pos_tables(n_tok, dim):
    nf = dim // 4
    omega = 1.0 / (10000.0 ** (jnp.arange(nf, dtype=F32) / nf))
    er = jnp.arange(n_tok // GRID_W, dtype=F32)[:, None] * omega
    ec = jnp.arange(GRID_W, dtype=F32)[:, None] * omega
    return (jnp.concatenate([jnp.sin(er), jnp.cos(er)], axis=-1),
            jnp.concatenate([jnp.sin(ec), jnp.cos(ec)], axis=-1))


def _lower_bounds(p):
    p = jax.nn.softmax(p.astype(F32), axis=0)
    return jnp.maximum(jnp.cumsum(p, axis=0) - p[0], 0.0)


def kernel(x_prompt, x_sample, c, state_hgrn, c_ctx, w_mod, b_mod, norm_mix, norm_mlp, hgrn_w_in, hgrn_lb_fwd, hgrn_lb_bwd, hgrn_g_norm, hgrn_w_out, conv_w_pw1, conv_w_dw, conv_b_dw, conv_ln_g, conv_ln_b, conv_w_pw2, mlp_w1, mlp_w2, final_norm):
    ctx_b, ctx_t, d = x_prompt.shape
    lat_b, lat_t, _ = x_sample.shape
    depth = w_mod.shape[0]
    d_ff = mlp_w1.shape[2]
    n_lat, n_ctx = lat_b * lat_t, ctx_b * ctx_t
    n_tok = n_lat + n_ctx
    assert d == HEADS * HEAD_DIM and lat_b < MOD_ROWS
    assert depth % 2 == 0
    assert lat_t % TM == 0 and n_ctx % TM == 0 and ctx_t % (STEP_CHUNKS * CHUNK) == 0 and ctx_t == TM_CONV and d_ff % TF == 0
    assert TM % GRID_W == 0 and (TM // GRID_W) % SUBLANES == 0

    def mod_row(tm):
        return lambda i, *_: (jnp.where(i * tm < n_lat, (i * tm) // lat_t, lat_b), 0, 0)

    def mod_spec(tm):
        return pl.BlockSpec((1, N_MOD, d), mod_row(tm))

    def tok_spec(tm, width=d):
        return pl.BlockSpec((tm, width), lambda i, *_: (i, 0))

    def full_spec(shape):
        return pl.BlockSpec(shape, lambda i, *_: (0,) * len(shape))

    def resident_spec(shape):
        return pl.BlockSpec(shape, lambda i, *_: (0,) * len(shape), pipeline_mode=pl.Buffered(1))

    row = lambda a: a.reshape(1, -1)
    tok_f32 = jax.ShapeDtypeStruct((n_tok, d), F32)
    tok_bf16 = jax.ShapeDtypeStruct((n_tok, d), BF16)

    cvec = jnp.zeros((MOD_ROWS, d), F32).at[:lat_b].set(c).at[lat_b].set(c_ctx)
    mods = _modulation(cvec, w_mod, b_mod).reshape(depth, MOD_ROWS, N_MOD, d)

    lbs_f = _lower_bounds(hgrn_lb_fwd)
    lbs_b = _lower_bounds(hgrn_lb_bwd)

    pos_row, pos_col = _grid_pos_tables(lat_t, d)
    n_lat_tiles = n_lat // TM
    tiles_per_lat_seq = lat_t // TM
    first_specs = [pl.BlockSpec((TM, d), lambda i: (jnp.minimum(i, n_lat_tiles - 1), 0)),
                   pl.BlockSpec((TM // GRID_W, d // 2), lambda i: (i % tiles_per_lat_seq, 0)),
                   full_spec((GRID_W, d // 2)),
                   pl.BlockSpec((TM, d), lambda i: (jnp.maximum(i - n_lat_tiles, 0), 0))]
    x = None

    new_states = None
    for i in range(depth):
        mod = mods[i]
        if i % 2 == 0:
            a = i // 2
            first = i == 0
            x_args = ((x_sample.reshape(n_lat, d), pos_row, pos_col, x_prompt.reshape(n_ctx, d))
                      if first else (x,))
            outs = pl.pallas_call(
                functools.partial(_hgrn_in_kernel, first=first, n_lat_tiles=n_lat_tiles),
                grid=(n_tok // TM,),
                in_specs=(first_specs if first else [tok_spec(TM)]) + [
                    mod_spec(TM), full_spec((1, d)), resident_spec((d, 5 * d)),
                    full_spec((1, d)), full_spec((1, d)), full_spec((1, d))],
                out_specs=[tok_spec(TM)] * (8 if first else 7),
                out_shape=[tok_bf16, tok_bf16, tok_bf16, tok_f32, tok_bf16, tok_f32, tok_bf16]
                + ([tok_f32] if first else []),
                compiler_params=_params("arbitrary"),
                name="hgrn_in",
            )(*x_args, mod, row(norm_mix[i]), hgrn_w_in[a].astype(BF16), row(lbs_f[a]), row(lbs_b[a]),
              row(hgrn_g_norm[a]))
            q, v, kf, bf, kb, bb, g = outs[:7]
            if first:
                x = outs[7]
            of, ob, new_states = _scan(q, kf, bf, kb, bb, v, state_hgrn, new_states, a, lat_b=lat_b, lat_t=lat_t,
                                       ctx_b=ctx_b, ctx_t=ctx_t)
            x = pl.pallas_call(
                _hgrn_out_mlp_kernel,
                grid=(n_tok // TM,),
                in_specs=[tok_spec(TM), tok_spec(TM), tok_spec(TM), tok_spec(TM), mod_spec(TM),
                          full_spec((d, d)), full_spec((1, d)), resident_spec((d, d_ff)),
                          resident_spec((d_ff, d))],
                out_specs=tok_spec(TM),
                out_shape=tok_f32,
                compiler_params=_params("parallel"),
                name="hgrn_out_mlp",
            )(of, ob, g, x, mod, hgrn_w_out[a].astype(BF16), row(norm_mlp[i]), mlp_w1[i].astype(BF16),
              mlp_w2[i].astype(BF16))
            continue

        b = i // 2
        u = pl.pallas_call(
            _conv_in_kernel,
            grid=(n_tok // TM,),
            in_specs=[tok_spec(TM), mod_spec(TM), full_spec((1, d)), full_spec((d, 2 * d))],
            out_specs=tok_spec(TM),
            out_shape=tok_f32,
            compiler_params=_params("parallel"),
            name="conv_in",
        )(x, mod, row(norm_mix[i]), conv_w_pw1[b].astype(BF16))
        tm = TM_CONV
        hb = tm // HALO
        n_halo = n_tok // HALO
        wdw = jnp.zeros((CONV_K + 1, d), F32).at[:CONV_K].set(conv_w_dw[b])
        x = pl.pallas_call(
            functools.partial(_conv_out_kernel, n_lat_tiles=n_lat // tm, tiles_per_lat_seq=lat_t // tm),
            grid=(n_tok // tm,),
            in_specs=[pl.BlockSpec((HALO, d), lambda i: (jnp.maximum(i * hb - 1, 0), 0)),
                      tok_spec(tm),
                      pl.BlockSpec((HALO, d), lambda i: (jnp.minimum((i + 1) * hb, n_halo - 1), 0)),
                      tok_spec(tm), mod_spec(tm), full_spec((CONV_K + 1, d)), full_spec((1, d)),
                      full_spec((1, d)), full_spec((1, d)), full_spec((d, d))],
            out_specs=tok_spec(tm),
            out_shape=tok_f32,
            scratch_shapes=[pltpu.VMEM((tm + 2 * HALO, d), F32),
                            pltpu.VMEM((SUBLANES - 1, tm + 2 * HALO - SUBLANES, d), F32),
                            pltpu.VMEM((CONV_K, SUBLANES, d), F32),
                            pltpu.VMEM((tm, d), F32)],
            compiler_params=_params("arbitrary"),
            name="conv_out",
        )(u, u, u, x, mod, wdw, row(conv_b_dw[b]), row(conv_ln_g[b]), row(conv_ln_b[b]),
          conv_w_pw2[b].astype(BF16))

        final = i == depth - 1
        if final:
            out_specs = [pl.BlockSpec((TM, d), lambda i: (jnp.minimum(i, n_lat_tiles - 1), 0)),
                         pl.BlockSpec((TM, d), lambda i: (jnp.maximum(i - n_lat_tiles, 0), 0))]
            out_shape = [jax.ShapeDtypeStruct((n_lat, d), F32), jax.ShapeDtypeStruct((n_ctx, d), F32)]
        else:
            out_specs, out_shape = tok_spec(TM), tok_f32
        x = pl.pallas_call(
            functools.partial(_mlp_kernel, final=final, n_lat_tiles=n_lat_tiles),
            grid=(n_tok // TM,),
            in_specs=[tok_spec(TM), mod_spec(TM), full_spec((1, d)), full_spec((1, d)),
                      resident_spec((d, d_ff)), resident_spec((d_ff, d))],
            out_specs=out_specs,
            out_shape=out_shape,
            compiler_params=_params("arbitrary"),
            name="mlp",
        )(x, mod, row(norm_mlp[i]), row(final_norm), mlp_w1[i].astype(BF16), mlp_w2[i].astype(BF16))

    y_lat, y_ctx = x
    y_sample = y_lat.reshape(lat_b, lat_t, d)
    y_prompt = y_ctx.reshape(ctx_b, ctx_t, d)
    return (y_prompt, y_sample, new_states.astype(x_prompt.dtype))
```
